```python
import math
import jax, jax.numpy as jnp
from jax import lax
import numpy as np

D_MODEL = 2048
BATCH = 4
SEQ = 8192
DEPTH = 4

N_MIXERS = 2
D_FF = 5632
ROPE_THETA = 10000.0
Q_BLOCK = 128
NORM_EPS = 1e-6
N_ADA = 9
A_HEAD_DIM = 128
A_HEADS = D_MODEL // (2 * A_HEAD_DIM)
A_IN = A_HEADS * (2 * A_HEAD_DIM) * 3
B_HEAD_DIM = 128
B_HEADS = D_MODEL // B_HEAD_DIM
IDX_HEADS = 16
IDX_HEAD_DIM = 128
IDX_ROPE_DIM = 64
TOPK_MAX = 256
B_IN = B_HEADS * B_HEAD_DIM + 2 * B_HEAD_DIM + IDX_HEADS * IDX_HEAD_DIM + IDX_HEAD_DIM + IDX_HEADS
N_A_LAYERS = (DEPTH + 1) // 2
N_B_LAYERS = DEPTH // 2

kernel_name = "hybrid_diffattn_dsa_macaron_adaln"


def rms_norm(x, eps=NORM_EPS):
    xf = x.astype(jnp.float32)
    return (xf * lax.rsqrt(jnp.mean(xf * xf, axis=-1, keepdims=True) + eps)).astype(x.dtype)


def layer_norm(x, g, b, eps=NORM_EPS):
    xf = x.astype(jnp.float32)
    mu = jnp.mean(xf, axis=-1, keepdims=True)
    var = jnp.mean(jnp.square(xf - mu), axis=-1, keepdims=True)
    return ((xf - mu) * lax.rsqrt(var + eps)).astype(x.dtype) * g + b


def rope_tables(seq, dim):
    pos = jnp.arange(seq, dtype=jnp.float32)
    inv = 1.0 / (ROPE_THETA ** (jnp.arange(0, dim, 2, dtype=jnp.float32) / dim))
    ang = pos[:, None] * inv[None, :]
    return jnp.cos(ang), jnp.sin(ang)


def rope(x, cos, sin):
    half = x.shape[-1] // 2
    shape = (1, x.shape[1]) + (1,) * (x.ndim - 3) + (half,)
    c = cos.reshape(shape).astype(x.dtype)
    s = sin.reshape(shape).astype(x.dtype)
    x1, x2 = x[..., :half], x[..., half:]
    return jnp.concatenate([x1 * c - x2 * s, x2 * c + x1 * s], axis=-1)


def partial_rope(x, cos, sin):
    return jnp.concatenate([rope(x[..., :IDX_ROPE_DIM], cos, sin), x[..., IDX_ROPE_DIM:]], axis=-1)


def to_blocks(t):
    b, s = t.shape[:2]
    t = t.reshape((b, s // Q_BLOCK, Q_BLOCK) + t.shape[2:])
    return jnp.moveaxis(t, 1, 0)


def from_blocks(t):
    t = jnp.moveaxis(t, 0, 1)
    return t.reshape((t.shape[0], t.shape[1] * t.shape[2]) + t.shape[3:])


def swiglu(h, w_gate, w_up, w_down):
    return (jax.nn.silu(h @ w_gate) * (h @ w_up)) @ w_down


def modulate(x, shift, scale):
    return rms_norm(x) * (1 + scale) + shift


def diff_attention(h, w_in, w_out, q_gain, k_gain, lq1, lk1, lq2, lk2, subln_gain, cos, sin, lam_init):
    b, s, _ = h.shape
    hd = A_HEAD_DIM
    qk_w = A_HEADS * 2 * hd
    proj = h @ w_in
    q = proj[..., :qk_w].reshape(b, s, A_HEADS, 2, hd)
    k = proj[..., qk_w:2 * qk_w].reshape(b, s, A_HEADS, 2, hd)
    v = proj[..., 2 * qk_w:].reshape(b, s, A_HEADS, 2 * hd)
    q = rope(rms_norm(q) * q_gain, cos, sin)
    k = rope(rms_norm(k) * k_gain, cos, sin)
    f32 = jnp.float32
    lam = (jnp.exp(jnp.sum(lq1.astype(f32) * lk1.astype(f32)))
           - jnp.exp(jnp.sum(lq2.astype(f32) * lk2.astype(f32))) + lam_init)
    key_pos = jnp.arange(s)
    scale = hd ** -0.5

    def block(args):
        q_b, start = args
        qpos = start + jnp.arange(Q_BLOCK)
        sc = jnp.einsum('bqhcd,bkhcd->bhcqk', q_b, k).astype(f32) * scale
        mask = key_pos[None, :] <= qpos[:, None]
        p = jax.nn.softmax(jnp.where(mask, sc, -jnp.inf), axis=-1)
        a = p[:, :, 0] - lam * p[:, :, 1]
        return jnp.einsum('bhqk,bkhe->bqhe', a.astype(v.dtype), v)

    starts = jnp.arange(s // Q_BLOCK) * Q_BLOCK
    o = from_blocks(lax.map(block, (to_blocks(q), starts)))
    o = rms_norm(o) * subln_gain * (1.0 - lam_init)
    return o.reshape(b, s, A_HEADS * 2 * hd) @ w_out


def dsa_attention(h, w_in, w_out, q_gain, k_gain, kidx_gain, kidx_bias, cos, sin, cos_i, sin_i):
    b, s, _ = h.shape
    dh = B_HEAD_DIM
    q_w = B_HEADS * dh
    qi_w = IDX_HEADS * IDX_HEAD_DIM
    cuts = [q_w, q_w + dh, q_w + 2 * dh, q_w + 2 * dh + qi_w, q_w + 2 * dh + qi_w + IDX_HEAD_DIM]
    proj = h @ w_in
    q, k, v, qi, ki, wi = jnp.split(proj, cuts, axis=-1)
    q = rope(rms_norm(q.reshape(b, s, B_HEADS, dh)) * q_gain, cos, sin)
    k = rope(rms_norm(k) * k_gain, cos, sin)
    kv = jnp.concatenate([k, v], axis=-1)
    qi = partial_rope(qi.reshape(b, s, IDX_HEADS, IDX_HEAD_DIM), cos_i, sin_i)
    ki = partial_rope(layer_norm(ki, kidx_gain, kidx_bias), cos_i, sin_i)
    wi = wi.astype(jnp.float32) * (IDX_HEADS ** -0.5) * (IDX_HEAD_DIM ** -0.5)
    topk = min(TOPK_MAX, s // 4)
    key_pos = jnp.arange(s)
    scale = dh ** -0.5
    gather = jax.vmap(lambda t, i: t[i])

    def block(args):
        q_b, qi_b, wi_b, start = args
        qpos = start + jnp.arange(Q_BLOCK)
        rel = jax.nn.relu(jnp.einsum('bqhd,bkd->bqhk', qi_b, ki)).astype(jnp.float32)
        score = jnp.einsum('bqh,bqhk->bqk', wi_b, rel)
        score = jnp.where((key_pos[None, :] <= qpos[:, None])[None], score, -jnp.inf)
        _, idx = lax.top_k(score, topk)
        valid = idx <= qpos[None, :, None]
        kvg = gather(kv, idx)
        kg, vg = kvg[..., :dh], kvg[..., dh:]
        sc = jnp.einsum('bqhd,bqjd->bqhj', q_b, kg).astype(jnp.float32) * scale
        p = jax.nn.softmax(jnp.where(valid[:, :, None, :], sc, -jnp.inf), axis=-1)
        return jnp.einsum('bqhj,bqjd->bqhd', p.astype(vg.dtype), vg)

    starts = jnp.arange(s // Q_BLOCK) * Q_BLOCK
    o = from_blocks(lax.map(block, (to_blocks(q), to_blocks(qi), to_blocks(wi), starts)))
    return o.reshape(b, s, B_HEADS * dh) @ w_out


def setup_inputs(seed: int = 0) -> dict:
    key = jax.random.key(seed)
    ks = jax.random.split(key, 22)
    D = D_MODEL
    std = D ** -0.5
    nrm = jax.random.normal
    return {
        "x": nrm(ks[0], (BATCH, SEQ, D), jnp.float32),
        "c": nrm(ks[1], (BATCH, D), jnp.float32),
        "ada_w": nrm(ks[2], (DEPTH, D, N_ADA * D), jnp.float32) * (0.5 * std),
        "ada_b": 0.02 * nrm(ks[3], (DEPTH, N_ADA * D), jnp.float32),
        "ffn_w_gate": nrm(ks[4], (DEPTH, 2, D, D_FF), jnp.float32) * std,
        "ffn_w_up": nrm(ks[5], (DEPTH, 2, D, D_FF), jnp.float32) * std,
        "ffn_w_down": nrm(ks[6], (DEPTH, 2, D_FF, D), jnp.float32) * (D_FF ** -0.5),
        "a_w_in": nrm(ks[7], (N_A_LAYERS, D, A_IN), jnp.float32) * std,
        "a_w_out": nrm(ks[8], (N_A_LAYERS, A_HEADS * 2 * A_HEAD_DIM, D), jnp.float32) * ((A_HEADS * 2 * A_HEAD_DIM) ** -0.5),
        "a_q_gain": 1.0 + 0.02 * nrm(ks[9], (N_A_LAYERS, A_HEAD_DIM), jnp.float32),
        "a_k_gain": 1.0 + 0.02 * nrm(ks[10], (N_A_LAYERS, A_HEAD_DIM), jnp.float32),
        "a_lambda_q1": 0.1 * nrm(ks[11], (N_A_LAYERS, A_HEAD_DIM), jnp.float32),
        "a_lambda_k1": 0.1 * nrm(ks[12], (N_A_LAYERS, A_HEAD_DIM), jnp.float32),
        "a_lambda_q2": 0.1 * nrm(ks[13], (N_A_LAYERS, A_HEAD_DIM), jnp.float32),
        "a_lambda_k2": 0.1 * nrm(ks[14], (N_A_LAYERS, A_HEAD_DIM), jnp.float32),
        "a_subln_gain": 1.0 + 0.02 * nrm(ks[15], (N_A_LAYERS, 2 * A_HEAD_DIM), jnp.float32),
        "b_w_in": nrm(ks[16], (N_B_LAYERS, D, B_IN), jnp.float32) * std,
        "b_w_out": nrm(ks[17], (N_B_LAYERS, B_HEADS * B_HEAD_DIM, D), jnp.float32) * ((B_HEADS * B_HEAD_DIM) ** -0.5),
        "b_q_gain": 1.0 + 0.02 * nrm(ks[18], (N_B_LAYERS, B_HEAD_DIM), jnp.float32),
        "b_k_gain": 1.0 + 0.02 * nrm(ks[19], (N_B_LAYERS, B_HEAD_DIM), jnp.float32),
        "b_kidx_gain": 1.0 + 0.02 * nrm(ks[20], (N_B_LAYERS, IDX_HEAD_DIM), jnp.float32),
        "b_kidx_bias": 0.02 * nrm(ks[21], (N_B_LAYERS, IDX_HEAD_DIM), jnp.float32),
    }


def reference(x, c, ada_w, ada_b, ffn_w_gate, ffn_w_up, ffn_w_down,
              a_w_in, a_w_out, a_q_gain, a_k_gain, a_lambda_q1, a_lambda_k1,
              a_lambda_q2, a_lambda_k2, a_subln_gain,
              b_w_in, b_w_out, b_q_gain, b_k_gain, b_kidx_gain, b_kidx_bias):
    b, s, d = x.shape
    cos, sin = rope_tables(s, A_HEAD_DIM)
    cos_i, sin_i = rope_tables(s, IDX_ROPE_DIM)
    c_act = jax.nn.silu(c)
    for i in range(DEPTH):
        mod = (c_act @ ada_w[i] + ada_b[i]).reshape(b, N_ADA, 1, d)
        sh1, sc1, g1, sh2, sc2, g2, sh3, sc3, g3 = [mod[:, j] for j in range(N_ADA)]
        x = x + 0.5 * g1 * swiglu(modulate(x, sh1, sc1), ffn_w_gate[i, 0], ffn_w_up[i, 0], ffn_w_down[i, 0])
        h = modulate(x, sh2, sc2)
        j = i // N_MIXERS
        if i % N_MIXERS == 0:
            lam_init = 0.8 - 0.6 * math.exp(-0.3 * i)
            y = diff_attention(h, a_w_in[j], a_w_out[j], a_q_gain[j], a_k_gain[j],
                               a_lambda_q1[j], a_lambda_k1[j], a_lambda_q2[j], a_lambda_k2[j],
                               a_subln_gain[j], cos, sin, lam_init)
        else:
            y = dsa_attention(h, b_w_in[j], b_w_out[j], b_q_gain[j], b_k_gain[j],
                              b_kidx_gain[j], b_kidx_bias[j], cos, sin, cos_i, sin_i)
        x = x + g2 * y
        x = x + 0.5 * g3 * swiglu(modulate(x, sh3, sc3), ffn_w_gate[i, 1], ffn_w_up[i, 1], ffn_w_down[i, 1])
    return x
```

```python
import functools
import math

import jax
import jax.numpy as jnp
import numpy as np
from jax import lax
from jax.experimental import pallas as pl
from jax.experimental.pallas import tpu as pltpu

ROPE_THETA = 10000.0
NORM_EPS = 1e-6
N_ADA = 9
HEAD_DIM = 128
IDX_HEADS = 16
IDX_ROPE_DIM = 64
TOPK_MAX = 256
LANES = 128
V7X_VMEM_LIMIT = 56 * 1024 * 1024
NEG = -1e30
INT_MIN = np.int32(-(2 ** 31))

BF16 = jnp.bfloat16
F32 = jnp.float32


def _cparams(sem):
    return pltpu.CompilerParams(dimension_semantics=sem, vmem_limit_bytes=V7X_VMEM_LIMIT)


def _dot(a, b):
    return jnp.dot(a, b, preferred_element_type=F32)


def _dot_nt(a, b):
    return lax.dot_general(a, b, (((1,), (1,)), ((), ())), preferred_element_type=F32)


def _tile_lanes(x, n):
    return x if n == 1 else jnp.concatenate([x] * n, axis=1)


def _modulate(x, shift, scale):
    ms = jnp.mean(x * x, axis=-1, keepdims=True)
    return x * lax.rsqrt(ms + NORM_EPS) * (1.0 + scale) + shift


def _rms(x):
    return x * lax.rsqrt(jnp.mean(x * x, axis=-1, keepdims=True) + NORM_EPS)


def _ada_kernel(c_ref, w_ref, b_ref, o_ref):
    c = c_ref[...]
    ca = (c * (1.0 / (1.0 + jnp.exp(-c)))).astype(BF16)
    o_ref[0] = _dot(ca, w_ref[0].astype(BF16)) + b_ref[0]


def _ada_mod(c, ada_w, ada_b):
    depth, d, n = ada_w.shape
    b = c.shape[0]
    rows = 8
    c8 = jnp.zeros((rows, d), F32).at[:b].set(c)
    tn = math.gcd(n, 1024)
    out = pl.pallas_call(
        _ada_kernel,
        grid=(depth, n // tn),
        in_specs=[
            pl.BlockSpec((rows, d), lambda i, j: (0, 0)),
            pl.BlockSpec((1, d, tn), lambda i, j: (i, 0, j)),
            pl.BlockSpec((1, 1, tn), lambda i, j: (i, 0, j)),
        ],
        out_specs=pl.BlockSpec((1, rows, tn), lambda i, j: (i, 0, j)),
        out_shape=jax.ShapeDtypeStruct((depth, rows, n), F32),
        compiler_params=_cparams(("arbitrary", "arbitrary")),
        name="ada_mod",
    )(c8, ada_w, ada_b.reshape(depth, 1, n))
    return out[:, :b].reshape(depth, b, N_ADA, d)


def _ffn_kernel(x_ref, mod_ref, wg_ref, wu_ref, wd_ref, o_ref, h_ref, *, row0, nf):
    f = pl.program_id(1)

    @pl.when(f == 0)
    def _():
        h = _modulate(x_ref[...], mod_ref[0, row0:row0 + 1, :], mod_ref[0, row0 + 1:row0 + 2, :])
        h_ref[...] = h.astype(BF16)
        o_ref[...] = jnp.zeros_like(o_ref)

    h = h_ref[...]
    a = _dot(h, wg_ref[...])
    u = _dot(h, wu_ref[...])
    act = (a * (1.0 / (1.0 + jnp.exp(-a))) * u).astype(BF16)
    o_ref[...] += _dot(act, wd_ref[...])

    @pl.when(f == nf - 1)
    def _():
        g = mod_ref[0, row0 + 2:row0 + 3, :]
        o_ref[...] = x_ref[...] + 0.5 * g * o_ref[...]


def _ffn(x2, mod, wg, wu, wd, row0, seq):
    t, d = x2.shape
    f_dim = wg.shape[1]
    tm = min(512, seq)
    tf = 512 if f_dim % 512 == 0 else f_dim
    nsb = seq // tm
    nf = f_dim // tf
    return pl.pallas_call(
        functools.partial(_ffn_kernel, row0=row0, nf=nf),
        grid=(t // tm, nf),
        in_specs=[
            pl.BlockSpec((tm, d), lambda i, f: (i, 0)),
            pl.BlockSpec((1, N_ADA, d), lambda i, f: (i // nsb, 0, 0)),
            pl.BlockSpec((d, tf), lambda i, f: (0, f)),
            pl.BlockSpec((d, tf), lambda i, f: (0, f)),
            pl.BlockSpec((tf, d), lambda i, f: (f, 0)),
        ],
        out_specs=pl.BlockSpec((tm, d), lambda i, f: (i, 0)),
        out_shape=jax.ShapeDtypeStruct((t, d), F32),
        scratch_shapes=[pltpu.VMEM((tm, d), BF16)],
        compiler_params=_cparams(("arbitrary", "arbitrary")),
        name="ffn",
    )(x2, mod, wg, wu, wd)


def _rope_full(r, cos, sin):
    return r * cos + pltpu.roll(r, HEAD_DIM // 2, 1) * sin


def _rope_partial(r, ci, sa, sb):
    q = IDX_ROPE_DIM // 2
    return r * ci + pltpu.roll(r, LANES - q, 1) * sa + pltpu.roll(r, q, 1) * sb


def _proj_a_kernel(x_ref, mod_ref, w_ref, gain_ref, cos_ref, sin_ref, o_ref, h_ref, *, n_qk, tn):
    j = pl.program_id(1)

    @pl.when(j == 0)
    def _():
        h = _modulate(x_ref[...], mod_ref[0, 3:4, :], mod_ref[0, 4:5, :])
        h_ref[...] = h.astype(BF16)

    p = _dot(h_ref[...], w_ref[...])

    @pl.when(j < n_qk)
    def _():
        cos = cos_ref[...]
        sin = sin_ref[...]
        for c in range(tn // LANES):
            sl = slice(c * LANES, (c + 1) * LANES)
            r = _rms(p[:, sl]) * gain_ref[0, :, sl]
            o_ref[:, sl] = _rope_full(r, cos, sin).astype(BF16)

    @pl.when(j >= n_qk)
    def _():
        o_ref[...] = p.astype(BF16)


def _proj_a(x2, mod, w, gains, cos, sin, seq):
    t, d = x2.shape
    n = w.shape[1]
    tm = min(512, seq)
    tn = 512
    nsb = seq // tm
    n_qk = 2 * d // tn
    return pl.pallas_call(
        functools.partial(_proj_a_kernel, n_qk=n_qk, tn=tn),
        grid=(t // tm, n // tn),
        in_specs=[
            pl.BlockSpec((tm, d), lambda i, j: (i, 0)),
            pl.BlockSpec((1, N_ADA, d), lambda i, j: (i // nsb, 0, 0)),
            pl.BlockSpec((d, tn), lambda i, j: (0, j)),
            pl.BlockSpec((1, 1, tn), lambda i, j: (j, 0, 0)),
            pl.BlockSpec((tm, LANES), lambda i, j: (i % nsb, 0)),
            pl.BlockSpec((tm, LANES), lambda i, j: (i % nsb, 0)),
        ],
        out_specs=pl.BlockSpec((tm, tn), lambda i, j: (i, j)),
        out_shape=jax.ShapeDtypeStruct((t, n), BF16),
        scratch_shapes=[pltpu.VMEM((tm, d), BF16)],
        compiler_params=_cparams(("arbitrary", "arbitrary")),
        name="proj_a",
    )(x2, mod, w, gains, cos, sin)


def _proj_b_kernel(x_ref, mod_ref, w_ref, gain_ref, cos_ref, sin_ref, ci_ref, sa_ref, sb_ref,
                   o_ref, wi_ref, h_ref, *, n_q, n_qi, tn, wi_scale):
    j = pl.program_id(1)

    @pl.when(j == 0)
    def _():
        h = _modulate(x_ref[...], mod_ref[0, 3:4, :], mod_ref[0, 4:5, :])
        h_ref[...] = h.astype(BF16)

    p = _dot(h_ref[...], w_ref[...])

    @pl.when(j < n_qi)
    def _():
        ci = ci_ref[...]
        sa = sa_ref[...]
        sb = sb_ref[...]
        for c in range(tn // LANES):
            sl = slice(c * LANES, (c + 1) * LANES)
            o_ref[:, sl] = _rope_partial(p[:, sl], ci, sa, sb).astype(BF16)

    @pl.when(jnp.logical_and(j >= n_qi, j < n_q + n_qi))
    def _():
        cos = cos_ref[...]
        sin = sin_ref[...]
        for c in range(tn // LANES):
            sl = slice(c * LANES, (c + 1) * LANES)
            r = _rms(p[:, sl]) * gain_ref[0, 0:1, sl]
            o_ref[:, sl] = _rope_full(r, cos, sin).astype(BF16)

    @pl.when(j == n_q + n_qi)
    def _():
        k = _rms(p[:, 0:LANES]) * gain_ref[0, 0:1, 0:LANES]
        o_ref[:, 0:LANES] = _rope_full(k, cos_ref[...], sin_ref[...]).astype(BF16)
        o_ref[:, LANES:2 * LANES] = p[:, LANES:2 * LANES].astype(BF16)
        ki = p[:, 2 * LANES:3 * LANES]
        mu = jnp.mean(ki, axis=-1, keepdims=True)
        kc = ki - mu
        var = jnp.mean(kc * kc, axis=-1, keepdims=True)
        kn = kc * lax.rsqrt(var + NORM_EPS) * gain_ref[0, 0:1, 2 * LANES:3 * LANES] \
            + gain_ref[0, 1:2, 2 * LANES:3 * LANES]
        o_ref[:, 2 * LANES:3 * LANES] = _rope_partial(kn, ci_ref[...], sa_ref[...], sb_ref[...]).astype(BF16)
        wi = p[:, 3 * LANES:4 * LANES] * wi_scale
        o_ref[:, 3 * LANES:4 * LANES] = wi.astype(BF16)
        wi_ref[...] = wi


def _proj_b(x2, mod, w, gains, cos, sin, ci, sa, sb, seq):
    t, d = x2.shape
    n = w.shape[1]
    tm = min(512, seq)
    tn = 512
    nsb = seq // tm
    n_q = d // tn
    n_qi = IDX_HEADS * HEAD_DIM // tn
    assert n == (n_q + n_qi + 1) * tn
    wi_scale = (IDX_HEADS ** -0.5) * (HEAD_DIM ** -0.5)
    tab = pl.BlockSpec((tm, LANES), lambda i, j: (i % nsb, 0))
    return pl.pallas_call(
        functools.partial(_proj_b_kernel, n_q=n_q, n_qi=n_qi, tn=tn, wi_scale=wi_scale),
        grid=(t // tm, n // tn),
        in_specs=[
            pl.BlockSpec((tm, d), lambda i, j: (i, 0)),
            pl.BlockSpec((1, N_ADA, d), lambda i, j: (i // nsb, 0, 0)),
            pl.BlockSpec((d, tn), lambda i, j: (0, j)),
            pl.BlockSpec((1, 2, tn), lambda i, j: (j, 0, 0)),
            tab, tab, tab, tab, tab,
        ],
        out_specs=[
            pl.BlockSpec((tm, tn), lambda i, j: (i, j)),
            pl.BlockSpec((tm, LANES), lambda i, j: (i, 0)),
        ],
        out_shape=[
            jax.ShapeDtypeStruct((t, n), BF16),
            jax.ShapeDtypeStruct((t, LANES), F32),
        ],
        scratch_shapes=[pltpu.VMEM((tm, d), BF16)],
        compiler_params=_cparams(("arbitrary", "arbitrary")),
        name="proj_b",
    )(x2, mod, w, gains, cos, sin, ci, sa, sb)


def _out_kernel(o_ref, w_ref, x_ref, mod_ref, y_ref):
    y = _dot(o_ref[...], w_ref[...])
    y_ref[...] = x_ref[...] + mod_ref[0, 5:6, :] * y


def _out_proj(o2, w, x2, mod, seq):
    t, d = x2.shape
    k = o2.shape[1]
    tm = min(512, seq)
    nsb = seq // tm
    return pl.pallas_call(
        _out_kernel,
        grid=(t // tm,),
        in_specs=[
            pl.BlockSpec((tm, k), lambda i: (i, 0)),
            pl.BlockSpec((k, d), lambda i: (0, 0)),
            pl.BlockSpec((tm, d), lambda i: (i, 0)),
            pl.BlockSpec((1, N_ADA, d), lambda i: (i // nsb, 0, 0)),
        ],
        out_specs=pl.BlockSpec((tm, d), lambda i: (i, 0)),
        out_shape=jax.ShapeDtypeStruct((t, d), F32),
        compiler_params=_cparams(("arbitrary",)),
        name="out_proj",
    )(o2, w, x2, mod)


def _softmax_step(s, m_ref, l_ref, acc_ref, v, idx):
    tk = s.shape[1]
    dv = v.shape[1]
    m_prev = m_ref[idx]
    m_new = jnp.maximum(m_prev, jnp.max(s, axis=1, keepdims=True))
    alpha = jnp.exp(m_prev - m_new)
    p = jnp.exp(s - _tile_lanes(m_new, tk // LANES))
    l_ref[idx] = alpha * l_ref[idx] + jnp.sum(p, axis=1, keepdims=True)
    acc_ref[idx] = _tile_lanes(alpha, dv // LANES) * acc_ref[idx] + _dot(p.astype(BF16), v)
    m_ref[idx] = m_new


def _attn_a_kernel(q0_ref, q1_ref, k0_ref, k1_ref, v_ref, lam_ref, gain_ref, o_ref,
                   m_ref, l_ref, acc_ref, *, tq, tk, lam_init):
    qi = pl.program_id(2)
    m_ref[...] = jnp.full_like(m_ref, NEG)
    l_ref[...] = jnp.zeros_like(l_ref)
    acc_ref[...] = jnp.zeros_like(acc_ref)
    q0 = q0_ref[0]
    q1 = q1_ref[0]
    n_full = (qi * tq) // tk

    def chunk(c, masked):
        off = pl.multiple_of(c * tk, tk)
        v = v_ref[0, pl.ds(off, tk), :]
        s0 = _dot_nt(q0, k0_ref[0, pl.ds(off, tk), :])
        s1 = _dot_nt(q1, k1_ref[0, pl.ds(off, tk), :])
        if masked:
            row = qi * tq + lax.broadcasted_iota(jnp.int32, (tq, tk), 0)
            col = off + lax.broadcasted_iota(jnp.int32, (tq, tk), 1)
            keep = col <= row
            s0 = jnp.where(keep, s0, NEG)
            s1 = jnp.where(keep, s1, NEG)
        _softmax_step(s0, m_ref, l_ref, acc_ref, v, 0)
        _softmax_step(s1, m_ref, l_ref, acc_ref, v, 1)

    def body(c, carry):
        chunk(c, False)
        return carry

    lax.fori_loop(0, n_full, body, 0)
    chunk(n_full, True)

    dv = acc_ref.shape[2]
    lam_q = lam_ref[...]
    s1 = jnp.sum(lam_q[0:1, :] * lam_q[1:2, :], axis=1, keepdims=True)
    s2 = jnp.sum(lam_q[2:3, :] * lam_q[3:4, :], axis=1, keepdims=True)
    lam = jnp.exp(s1) - jnp.exp(s2) + lam_init
    o0 = acc_ref[0] * _tile_lanes(1.0 / l_ref[0], dv // LANES)
    o1 = acc_ref[1] * _tile_lanes(1.0 / l_ref[1], dv // LANES)
    o = o0 - lam * o1
    o_ref[0] = (_rms(o) * gain_ref[...]).astype(BF16)


def _attn_a(p3, lam_vecs, gain_eff, n_heads, lam_init):
    b, s, n = p3.shape
    d = n // 3
    tq = min(256, s)
    tk = min(512, s)
    hd = HEAD_DIM
    kcol = d // hd
    vcol = 2 * d // (2 * hd)
    return pl.pallas_call(
        functools.partial(_attn_a_kernel, tq=tq, tk=tk, lam_init=lam_init),
        grid=(b, n_heads, s // tq),
        in_specs=[
            pl.BlockSpec((1, tq, hd), lambda bi, h, i: (bi, i, 2 * h)),
            pl.BlockSpec((1, tq, hd), lambda bi, h, i: (bi, i, 2 * h + 1)),
            pl.BlockSpec((1, s, hd), lambda bi, h, i: (bi, 0, kcol + 2 * h)),
            pl.BlockSpec((1, s, hd), lambda bi, h, i: (bi, 0, kcol + 2 * h + 1)),
            pl.BlockSpec((1, s, 2 * hd), lambda bi, h, i: (bi, 0, vcol + h)),
            pl.BlockSpec((4, hd), lambda bi, h, i: (0, 0)),
            pl.BlockSpec((1, 2 * hd), lambda bi, h, i: (0, 0)),
        ],
        out_specs=pl.BlockSpec((1, tq, 2 * hd), lambda bi, h, i: (bi, i, h)),
        out_shape=jax.ShapeDtypeStruct((b, s, d), BF16),
        scratch_shapes=[
            pltpu.VMEM((2, tq, LANES), F32),
            pltpu.VMEM((2, tq, LANES), F32),
            pltpu.VMEM((2, tq, 2 * hd), F32),
        ],
        compiler_params=_cparams(("arbitrary", "arbitrary", "arbitrary")),
        name="attn_a",
    )(p3, p3, p3, p3, p3, lam_vecs, gain_eff)


def _attn_b_kernel(q_ref, qi_ref, wi_ref, k_ref, v_ref, ki_ref, o_ref,
                   keys_ref, qall_ref, wib_ref, m_ref, l_ref, acc_ref,
                   *, tq, tk, n_heads, group, topk, idx_bits):
    i = pl.program_id(1)
    nkc = ((i + 1) * tq + tk - 1) // tk
    nrep = tk // LANES
    kf = float(topk)

    for h in range(n_heads):
        qall_ref[h * tq:(h + 1) * tq, :] = q_ref[0, :, h * HEAD_DIM:(h + 1) * HEAD_DIM]
    wi = wi_ref[0]
    for h in range(IDX_HEADS):
        wib_ref[h] = jnp.broadcast_to(wi[:, h:h + 1], (tq, LANES))

    def score_chunk(c, carry):
        off = pl.multiple_of(c * tk, tk)
        kic = ki_ref[0, pl.ds(off, tk), :]
        sc = jnp.zeros((tq, tk), F32)
        for h in range(IDX_HEADS):
            s = _dot_nt(qi_ref[0, :, h * HEAD_DIM:(h + 1) * HEAD_DIM], kic)
            sc = sc + _tile_lanes(wib_ref[h], nrep) * jnp.maximum(s, 0.0)
        bits = pltpu.bitcast(sc, jnp.int32)
        key = bits ^ ((bits >> 31) & jnp.int32(0x7FFFFFFF))
        row = i * tq + lax.broadcasted_iota(jnp.int32, (tq, tk), 0)
        col = off + lax.broadcasted_iota(jnp.int32, (tq, tk), 1)
        keys_ref[:, pl.ds(off, tk)] = jnp.where(col <= row, key, INT_MIN)
        return carry

    lax.fori_loop(0, nkc, score_chunk, 0)

    def row_total(part):
        return jnp.broadcast_to(jnp.sum(part.astype(F32), axis=1, keepdims=True), (tq, LANES))

    def count_ge(thr):
        def cbody(c, part):
            off = pl.multiple_of(c * tk, tk)
            kk = keys_ref[:, pl.ds(off, tk)]
            for r in range(nrep):
                part = part + jnp.where(kk[:, r * LANES:(r + 1) * LANES] >= thr, 1, 0)
            return part
        return row_total(lax.fori_loop(0, nkc, cbody, jnp.zeros((tq, LANES), jnp.int32)))

    def search(it, thr):
        cand = thr + jnp.left_shift(jnp.int32(1), 31 - it)
        return jnp.where(count_ge(cand) >= kf, cand, thr)

    thr = lax.fori_loop(0, 32, search, jnp.full((tq, LANES), INT_MIN, jnp.int32))

    tie = jnp.where(jnp.logical_and(count_ge(thr) > kf, thr > INT_MIN), 1.0, 0.0)
    any_tie = jnp.max(tie)

    @pl.when(any_tie > 0.0)
    def _():
        need = kf - count_ge(thr + 1)

        def count_eq_before(jcut):
            def cbody(c, part):
                off = pl.multiple_of(c * tk, tk)
                kk = keys_ref[:, pl.ds(off, tk)]
                for r in range(nrep):
                    col = off + r * LANES + lax.broadcasted_iota(jnp.int32, (tq, LANES), 1)
                    hit = jnp.logical_and(kk[:, r * LANES:(r + 1) * LANES] == thr, col < jcut)
                    part = part + jnp.where(hit, 1, 0)
                return part
            return row_total(lax.fori_loop(0, nkc, cbody, jnp.zeros((tq, LANES), jnp.int32)))

        def jsearch(it, jcut):
            cand = jcut + jnp.left_shift(jnp.int32(1), idx_bits - 1 - it)
            return jnp.where(count_eq_before(cand) <= need, cand, jcut)

        jcut = lax.fori_loop(0, idx_bits, jsearch, jnp.zeros((tq, LANES), jnp.int32))

        def demote(c, carry):
            off = pl.multiple_of(c * tk, tk)
            for r in range(nrep):
                cs = pl.ds(off + r * LANES, LANES)
                kk = keys_ref[:, cs]
                col = off + r * LANES + lax.broadcasted_iota(jnp.int32, (tq, LANES), 1)
                drop = jnp.logical_and(jnp.logical_and(kk == thr, col >= jcut), tie > 0.0)
                keys_ref[:, cs] = jnp.where(drop, kk - 1, kk)
            return carry

        lax.fori_loop(0, nkc, demote, 0)

    thr_sel = _tile_lanes(jnp.maximum(thr, INT_MIN + 1), nrep)
    m_ref[...] = jnp.full_like(m_ref, NEG)
    l_ref[...] = jnp.zeros_like(l_ref)
    acc_ref[...] = jnp.zeros_like(acc_ref)
    rows = group * tq

    def attend(c, carry):
        off = pl.multiple_of(c * tk, tk)
        kc = k_ref[0, pl.ds(off, tk), :]
        vc = v_ref[0, pl.ds(off, tk), :]
        bias = jnp.where(keys_ref[:, pl.ds(off, tk)] >= thr_sel, 0.0, NEG)
        for g in range(n_heads // group):
            rs = slice(g * rows, (g + 1) * rows)
            s = _dot_nt(qall_ref[rs, :], kc)
            s = (s.reshape(group, tq, tk) + bias[None]).reshape(rows, tk)
            m_prev = m_ref[rs, :]
            m_new = jnp.maximum(m_prev, jnp.max(s, axis=1, keepdims=True))
            alpha = jnp.exp(m_prev - m_new)
            p = jnp.exp(s - _tile_lanes(m_new, nrep))
            l_ref[rs, :] = alpha * l_ref[rs, :] + jnp.sum(p, axis=1, keepdims=True)
            acc_ref[rs, :] = alpha * acc_ref[rs, :] + _dot(p.astype(BF16), vc)
            m_ref[rs, :] = m_new
        return carry

    lax.fori_loop(0, nkc, attend, 0)

    for h in range(n_heads):
        hs = slice(h * tq, (h + 1) * tq)
        o_ref[0, :, h * HEAD_DIM:(h + 1) * HEAD_DIM] = (acc_ref[hs, :] * (1.0 / l_ref[hs, :])).astype(BF16)


def _attn_b(p3, wi3, n_heads, topk):
    b, s, n = p3.shape
    tq = 128
    tk = min(512, s)
    d = n_heads * HEAD_DIM
    qi_w = IDX_HEADS * HEAD_DIM
    assert qi_w % d == 0
    kcol = (d + qi_w) // HEAD_DIM
    group = min(4, n_heads)
    return pl.pallas_call(
        functools.partial(_attn_b_kernel, tq=tq, tk=tk, n_heads=n_heads, group=group, topk=topk,
                          idx_bits=int(s).bit_length()),
        grid=(b, s // tq),
        in_specs=[
            pl.BlockSpec((1, tq, d), lambda bi, i: (bi, i, qi_w // d)),
            pl.BlockSpec((1, tq, qi_w), lambda bi, i: (bi, i, 0)),
            pl.BlockSpec((1, tq, LANES), lambda bi, i: (bi, i, 0)),
            pl.BlockSpec((1, s, HEAD_DIM), lambda bi, i: (bi, 0, kcol)),
            pl.BlockSpec((1, s, HEAD_DIM), lambda bi, i: (bi, 0, kcol + 1)),
            pl.BlockSpec((1, s, HEAD_DIM), lambda bi, i: (bi, 0, kcol + 2)),
        ],
        out_specs=pl.BlockSpec((1, tq, d), lambda bi, i: (bi, i, 0)),
        out_shape=jax.ShapeDtypeStruct((b, s, d), BF16),
        scratch_shapes=[
            pltpu.VMEM((tq, s), jnp.int32),
            pltpu.VMEM((n_heads * tq, HEAD_DIM), BF16),
            pltpu.VMEM((IDX_HEADS, tq, LANES), F32),
            pltpu.VMEM((n_heads * tq, LANES), F32),
            pltpu.VMEM((n_heads * tq, LANES), F32),
            pltpu.VMEM((n_heads * tq, HEAD_DIM), F32),
        ],
        compiler_params=_cparams(("arbitrary", "arbitrary")),
        name="attn_b",
    )(p3, p3, wi3, p3, p3, p3)


def _rope_tables(seq):
    pos = jnp.arange(seq, dtype=F32)[:, None]

    def cs(dim):
        inv = 1.0 / (ROPE_THETA ** (jnp.arange(0, dim, 2, dtype=F32) / dim))
        ang = pos * inv[None, :]
        return jnp.cos(ang), jnp.sin(ang)

    c, s = cs(HEAD_DIM)
    cos = jnp.concatenate([c, c], axis=1)
    sin = jnp.concatenate([-s, s], axis=1)
    c, s = cs(IDX_ROPE_DIM)
    one = jnp.ones((seq, LANES - IDX_ROPE_DIM), F32)
    z32 = jnp.zeros_like(s)
    z64 = jnp.zeros_like(one)
    ci = jnp.concatenate([c, c, one], axis=1)
    sa = jnp.concatenate([-s, z32, z64], axis=1)
    sb = jnp.concatenate([z32, s, z64], axis=1)
    return cos, sin, ci, sa, sb


def kernel(x, c, ada_w, ada_b, ffn_w_gate, ffn_w_up, ffn_w_down, a_w_in, a_w_out, a_q_gain, a_k_gain,
           a_lambda_q1, a_lambda_k1, a_lambda_q2, a_lambda_k2, a_subln_gain, b_w_in, b_w_out, b_q_gain,
           b_k_gain, b_kidx_gain, b_kidx_bias):
    b, s, d = x.shape
    depth = ada_w.shape[0]
    hd = HEAD_DIM
    a_heads = d // (2 * hd)
    b_heads = d // hd
    scale = hd ** -0.5
    topk = min(TOPK_MAX, s // 4)
    tn = 512

    cos, sin, ci, sa, sb = _rope_tables(s)
    mod = _ada_mod(c, ada_w, ada_b)
    wg = ffn_w_gate.astype(BF16)
    wu = ffn_w_up.astype(BF16)
    wd = ffn_w_down.astype(BF16)

    x2 = x.reshape(b * s, d)
    for i in range(depth):
        j = i // 2
        x2 = _ffn(x2, mod[i], wg[i, 0], wu[i, 0], wd[i, 0], 0, s)
        if i % 2 == 0:
            lam_init = 0.8 - 0.6 * math.exp(-0.3 * i)
            reps = tn // hd
            gains = jnp.concatenate([
                jnp.tile(jnp.tile(a_q_gain[j] * scale, reps)[None], (d // tn, 1)),
                jnp.tile(jnp.tile(a_k_gain[j], reps)[None], (d // tn, 1)),
                jnp.ones((d // tn, tn), F32),
            ], axis=0)[:, None, :]
            p = _proj_a(x2, mod[i], a_w_in[j].astype(BF16), gains, cos, sin, s)
            lam_vecs = jnp.stack([a_lambda_q1[j], a_lambda_k1[j], a_lambda_q2[j], a_lambda_k2[j]])
            gain_eff = (a_subln_gain[j] * (1.0 - lam_init))[None, :]
            o = _attn_a(p.reshape(b, s, 3 * d), lam_vecs, gain_eff, a_heads, lam_init)
            x2 = _out_proj(o.reshape(b * s, d), a_w_out[j].astype(BF16), x2, mod[i], s)
        else:
            w = b_w_in[j]
            q_w = b_heads * hd
            qi_w = IDX_HEADS * hd
            o_k, o_v, o_qi, o_ki, o_wi = q_w, q_w + hd, q_w + 2 * hd, q_w + 2 * hd + qi_w, q_w + 3 * hd + qi_w
            w_re = jnp.concatenate([
                w[:, o_qi:o_qi + qi_w], w[:, :q_w], w[:, o_k:o_k + hd], w[:, o_v:o_v + hd],
                w[:, o_ki:o_ki + hd], w[:, o_wi:o_wi + IDX_HEADS],
                jnp.zeros((d, hd - IDX_HEADS), F32),
            ], axis=1).astype(BF16)
            n_tiles = w_re.shape[1] // tn
            reps = tn // hd
            g0 = jnp.zeros((n_tiles, 2, tn), F32)
            g0 = g0.at[qi_w // tn:(qi_w + q_w) // tn, 0, :].set(jnp.tile(b_q_gain[j] * scale, reps)[None])
            g0 = g0.at[n_tiles - 1, 0, 0:hd].set(b_k_gain[j])
            g0 = g0.at[n_tiles - 1, 0, 2 * hd:3 * hd].set(b_kidx_gain[j])
            g0 = g0.at[n_tiles - 1, 1, 2 * hd:3 * hd].set(b_kidx_bias[j])
            p, wi = _proj_b(x2, mod[i], w_re, g0, cos, sin, ci, sa, sb, s)
            o = _attn_b(p.reshape(b, s, w_re.shape[1]), wi.reshape(b, s, LANES), b_heads, topk)
            x2 = _out_proj(o.reshape(b * s, d), b_w_out[j].astype(BF16), x2, mod[i], s)
        x2 = _ffn(x2, mod[i], wg[i, 1], wu[i, 1], wd[i, 1], 6, s)
    return x2.reshape(b, s, d)
```

```python
import functools
import math

import jax
import jax.numpy as jnp
import numpy as np
from jax import lax
from jax.experimental import pallas as pl
from jax.experimental.pallas import tpu as pltpu

ROPE_THETA = 10000.0
NORM_EPS = 1e-6
N_ADA = 9
HEAD_DIM = 128
IDX_HEADS = 16
IDX_ROPE_DIM = 64
TOPK_MAX = 256
LANES = 128
SLAB = 128
V7X_VMEM_LIMIT = 56 * 1024 * 1024
NEG = -1e30
INT_MIN = np.int32(-(2 ** 31))

BF16 = jnp.bfloat16
F32 = jnp.float32


def _cparams(sem):
    return pltpu.CompilerParams(dimension_semantics=sem, vmem_limit_bytes=V7X_VMEM_LIMIT)


def _dot(a, b):
    return jnp.dot(a, b, preferred_element_type=F32)


def _dot_nt(a, b):
    return lax.dot_general(a, b, (((1,), (1,)), ((), ())), preferred_element_type=F32)


def _tile_lanes(x, n):
    return x if n == 1 else jnp.concatenate([x] * n, axis=1)


def _modulate(x, shift, scale):
    ms = jnp.mean(x * x, axis=-1, keepdims=True)
    return x * lax.rsqrt(ms + NORM_EPS) * (1.0 + scale) + shift


def _rms(x):
    return x * lax.rsqrt(jnp.mean(x * x, axis=-1, keepdims=True) + NORM_EPS)


def _ada_kernel(c_ref, w_ref, b_ref, o_ref):
    c = c_ref[...]
    ca = (c * (1.0 / (1.0 + jnp.exp(-c)))).astype(BF16)
    o_ref[0] = _dot(ca, w_ref[0].astype(BF16)) + b_ref[0]


def _ada_mod(c, ada_w, ada_b):
    depth, d, n = ada_w.shape
    b = c.shape[0]
    rows = 8
    c8 = jnp.zeros((rows, d), F32).at[:b].set(c)
    tn = math.gcd(n, 1024)
    out = pl.pallas_call(
        _ada_kernel,
        grid=(depth, n // tn),
        in_specs=[
            pl.BlockSpec((rows, d), lambda i, j: (0, 0)),
            pl.BlockSpec((1, d, tn), lambda i, j: (i, 0, j)),
            pl.BlockSpec((1, 1, tn), lambda i, j: (i, 0, j)),
        ],
        out_specs=pl.BlockSpec((1, rows, tn), lambda i, j: (i, 0, j)),
        out_shape=jax.ShapeDtypeStruct((depth, rows, n), F32),
        compiler_params=_cparams(("arbitrary", "arbitrary")),
        name="ada_mod",
    )(c8, ada_w, ada_b.reshape(depth, 1, n))
    return out[:, :b].reshape(depth, b, N_ADA, d)


def _ffn_kernel(x_ref, mod_ref, wg_ref, wu_ref, wd_ref, o_ref, h_ref, *, row0, nf):
    f = pl.program_id(1)

    @pl.when(f == 0)
    def _():
        h = _modulate(x_ref[...], mod_ref[0, row0:row0 + 1, :], mod_ref[0, row0 + 1:row0 + 2, :])
        h_ref[...] = h.astype(BF16)
        o_ref[...] = jnp.zeros_like(o_ref)

    h = h_ref[...]
    a = _dot(h, wg_ref[...])
    u = _dot(h, wu_ref[...])
    act = (a * (1.0 / (1.0 + jnp.exp(-a))) * u).astype(BF16)
    o_ref[...] += _dot(act, wd_ref[...])

    @pl.when(f == nf - 1)
    def _():
        g = mod_ref[0, row0 + 2:row0 + 3, :]
        o_ref[...] = x_ref[...] + 0.5 * g * o_ref[...]


def _ffn(x2, mod, wg, wu, wd, row0, seq):
    t, d = x2.shape
    f_dim = wg.shape[1]
    tm = min(512, seq)
    tf = 512 if f_dim % 512 == 0 else f_dim
    nsb = seq // tm
    nf = f_dim // tf
    return pl.pallas_call(
        functools.partial(_ffn_kernel, row0=row0, nf=nf),
        grid=(t // tm, nf),
        in_specs=[
            pl.BlockSpec((tm, d), lambda i, f: (i, 0)),
            pl.BlockSpec((1, N_ADA, d), lambda i, f: (i // nsb, 0, 0)),
            pl.BlockSpec((d, tf), lambda i, f: (0, f)),
            pl.BlockSpec((d, tf), lambda i, f: (0, f)),
            pl.BlockSpec((tf, d), lambda i, f: (f, 0)),
        ],
        out_specs=pl.BlockSpec((tm, d), lambda i, f: (i, 0)),
        out_shape=jax.ShapeDtypeStruct((t, d), F32),
        scratch_shapes=[pltpu.VMEM((tm, d), BF16)],
        compiler_params=_cparams(("arbitrary", "arbitrary")),
        name="ffn",
    )(x2, mod, wg, wu, wd)


def _rope_full(r, cos, sin):
    return r * cos + pltpu.roll(r, HEAD_DIM // 2, 1) * sin


def _rope_partial(r, ci, sa, sb):
    q = IDX_ROPE_DIM // 2
    return r * ci + pltpu.roll(r, LANES - q, 1) * sa + pltpu.roll(r, q, 1) * sb


def _proj_a_kernel(x_ref, mod_ref, w_ref, gain_ref, cos_ref, sin_ref, o_ref, h_ref, *, n_qk, tn):
    j = pl.program_id(1)

    @pl.when(j == 0)
    def _():
        h = _modulate(x_ref[...], mod_ref[0, 3:4, :], mod_ref[0, 4:5, :])
        h_ref[...] = h.astype(BF16)

    p = _dot(h_ref[...], w_ref[...])

    @pl.when(j < n_qk)
    def _():
        cos = cos_ref[...]
        sin = sin_ref[...]
        for c in range(tn // LANES):
            sl = slice(c * LANES, (c + 1) * LANES)
            r = _rms(p[:, sl]) * gain_ref[0, :, sl]
            o_ref[:, sl] = _rope_full(r, cos, sin).astype(BF16)

    @pl.when(j >= n_qk)
    def _():
        o_ref[...] = p.astype(BF16)


def _proj_a(x2, mod, w, gains, cos, sin, seq):
    t, d = x2.shape
    n = w.shape[1]
    tm = min(512, seq)
    tn = 512
    nsb = seq // tm
    n_qk = 2 * d // tn
    return pl.pallas_call(
        functools.partial(_proj_a_kernel, n_qk=n_qk, tn=tn),
        grid=(t // tm, n // tn),
        in_specs=[
            pl.BlockSpec((tm, d), lambda i, j: (i, 0)),
            pl.BlockSpec((1, N_ADA, d), lambda i, j: (i // nsb, 0, 0)),
            pl.BlockSpec((d, tn), lambda i, j: (0, j)),
            pl.BlockSpec((1, 1, tn), lambda i, j: (j, 0, 0)),
            pl.BlockSpec((tm, LANES), lambda i, j: (i % nsb, 0)),
            pl.BlockSpec((tm, LANES), lambda i, j: (i % nsb, 0)),
        ],
        out_specs=pl.BlockSpec((tm, tn), lambda i, j: (i, j)),
        out_shape=jax.ShapeDtypeStruct((t, n), BF16),
        scratch_shapes=[pltpu.VMEM((tm, d), BF16)],
        compiler_params=_cparams(("arbitrary", "arbitrary")),
        name="proj_a",
    )(x2, mod, w, gains, cos, sin)


def _proj_b_kernel(x_ref, mod_ref, w_ref, gain_ref, cos_ref, sin_ref, ci_ref, sa_ref, sb_ref,
                   o_ref, wi_ref, h_ref, *, n_q, n_qi, tn, wi_scale):
    j = pl.program_id(1)

    @pl.when(j == 0)
    def _():
        h = _modulate(x_ref[...], mod_ref[0, 3:4, :], mod_ref[0, 4:5, :])
        h_ref[...] = h.astype(BF16)

    p = _dot(h_ref[...], w_ref[...])

    @pl.when(j < n_qi)
    def _():
        ci = ci_ref[...]
        sa = sa_ref[...]
        sb = sb_ref[...]
        for c in range(tn // LANES):
            sl = slice(c * LANES, (c + 1) * LANES)
            o_ref[:, sl] = _rope_partial(p[:, sl], ci, sa, sb).astype(BF16)

    @pl.when(jnp.logical_and(j >= n_qi, j < n_q + n_qi))
    def _():
        cos = cos_ref[...]
        sin = sin_ref[...]
        for c in range(tn // LANES):
            sl = slice(c * LANES, (c + 1) * LANES)
            r = _rms(p[:, sl]) * gain_ref[0, 0:1, sl]
            o_ref[:, sl] = _rope_full(r, cos, sin).astype(BF16)

    @pl.when(j == n_q + n_qi)
    def _():
        k = _rms(p[:, 0:LANES]) * gain_ref[0, 0:1, 0:LANES]
        o_ref[:, 0:LANES] = _rope_full(k, cos_ref[...], sin_ref[...]).astype(BF16)
        ki = p[:, LANES:2 * LANES]
        mu = jnp.mean(ki, axis=-1, keepdims=True)
        kc = ki - mu
        var = jnp.mean(kc * kc, axis=-1, keepdims=True)
        kn = kc * lax.rsqrt(var + NORM_EPS) * gain_ref[0, 0:1, LANES:2 * LANES] \
            + gain_ref[0, 1:2, LANES:2 * LANES]
        o_ref[:, LANES:2 * LANES] = _rope_partial(kn, ci_ref[...], sa_ref[...], sb_ref[...]).astype(BF16)
        o_ref[:, 2 * LANES:3 * LANES] = p[:, 2 * LANES:3 * LANES].astype(BF16)
        o_ref[:, 3 * LANES:4 * LANES] = jnp.ones((p.shape[0], LANES), BF16)
        wi_ref[...] = p[:, 3 * LANES:4 * LANES] * wi_scale


def _proj_b(x2, mod, w, gains, cos, sin, ci, sa, sb, seq):
    t, d = x2.shape
    n = w.shape[1]
    tm = min(512, seq)
    tn = 512
    nsb = seq // tm
    n_q = d // tn
    n_qi = IDX_HEADS * HEAD_DIM // tn
    assert n == (n_q + n_qi + 1) * tn
    wi_scale = (IDX_HEADS ** -0.5) * (HEAD_DIM ** -0.5)
    tab = pl.BlockSpec((tm, LANES), lambda i, j: (i % nsb, 0))
    return pl.pallas_call(
        functools.partial(_proj_b_kernel, n_q=n_q, n_qi=n_qi, tn=tn, wi_scale=wi_scale),
        grid=(t // tm, n // tn),
        in_specs=[
            pl.BlockSpec((tm, d), lambda i, j: (i, 0)),
            pl.BlockSpec((1, N_ADA, d), lambda i, j: (i // nsb, 0, 0)),
            pl.BlockSpec((d, tn), lambda i, j: (0, j)),
            pl.BlockSpec((1, 2, tn), lambda i, j: (j, 0, 0)),
            tab, tab, tab, tab, tab,
        ],
        out_specs=[
            pl.BlockSpec((tm, tn), lambda i, j: (i, j)),
            pl.BlockSpec((tm, LANES), lambda i, j: (i, 0)),
        ],
        out_shape=[
            jax.ShapeDtypeStruct((t, n), BF16),
            jax.ShapeDtypeStruct((t, LANES), F32),
        ],
        scratch_shapes=[pltpu.VMEM((tm, d), BF16)],
        compiler_params=_cparams(("arbitrary", "arbitrary")),
        name="proj_b",
    )(x2, mod, w, gains, cos, sin, ci, sa, sb)


def _out_kernel(o_ref, w_ref, x_ref, mod_ref, y_ref):
    y = _dot(o_ref[...], w_ref[...])
    y_ref[...] = x_ref[...] + mod_ref[0, 5:6, :] * y


def _out_proj(o2, w, x2, mod, seq):
    t, d = x2.shape
    k = o2.shape[1]
    tm = min(512, seq)
    nsb = seq // tm
    return pl.pallas_call(
        _out_kernel,
        grid=(t // tm,),
        in_specs=[
            pl.BlockSpec((tm, k), lambda i: (i, 0)),
            pl.BlockSpec((k, d), lambda i: (0, 0)),
            pl.BlockSpec((tm, d), lambda i: (i, 0)),
            pl.BlockSpec((1, N_ADA, d), lambda i: (i // nsb, 0, 0)),
        ],
        out_specs=pl.BlockSpec((tm, d), lambda i: (i, 0)),
        out_shape=jax.ShapeDtypeStruct((t, d), F32),
        compiler_params=_cparams(("arbitrary",)),
        name="out_proj",
    )(o2, w, x2, mod)


def _softmax_step(s, m_ref, l_ref, acc_ref, v, idx):
    tk = s.shape[1]
    dv = v.shape[1]
    m_prev = m_ref[idx]
    m_new = jnp.maximum(m_prev, jnp.max(s, axis=1, keepdims=True))
    alpha = jnp.exp2(m_prev - m_new)
    p = jnp.exp2(s - _tile_lanes(m_new, tk // LANES))
    l_ref[idx] = alpha * l_ref[idx] + jnp.sum(p, axis=1, keepdims=True)
    acc_ref[idx] = _tile_lanes(alpha, dv // LANES) * acc_ref[idx] + _dot(p.astype(BF16), v)
    m_ref[idx] = m_new


def _attn_a_kernel(q0_ref, q1_ref, k0_ref, k1_ref, v_ref, lam_ref, gain_ref, o_ref,
                   m_ref, l_ref, acc_ref, *, tq, tk, lam_init):
    qi = pl.program_id(2)
    m_ref[...] = jnp.full_like(m_ref, NEG)
    l_ref[...] = jnp.zeros_like(l_ref)
    acc_ref[...] = jnp.zeros_like(acc_ref)
    q0 = q0_ref[0]
    q1 = q1_ref[0]
    n_full = (qi * tq) // tk

    def chunk(c, masked):
        off = pl.multiple_of(c * tk, tk)
        v = v_ref[0, pl.ds(off, tk), :]
        s0 = _dot_nt(q0, k0_ref[0, pl.ds(off, tk), :])
        s1 = _dot_nt(q1, k1_ref[0, pl.ds(off, tk), :])
        if masked:
            row = qi * tq + lax.broadcasted_iota(jnp.int32, (tq, tk), 0)
            col = off + lax.broadcasted_iota(jnp.int32, (tq, tk), 1)
            keep = col <= row
            s0 = jnp.where(keep, s0, NEG)
            s1 = jnp.where(keep, s1, NEG)
        _softmax_step(s0, m_ref, l_ref, acc_ref, v, 0)
        _softmax_step(s1, m_ref, l_ref, acc_ref, v, 1)

    def body(c, carry):
        chunk(c, False)
        return carry

    lax.fori_loop(0, n_full, body, 0)
    chunk(n_full, True)

    dv = acc_ref.shape[2]
    lam_q = lam_ref[...]
    s1 = jnp.sum(lam_q[0:1, :] * lam_q[1:2, :], axis=1, keepdims=True)
    s2 = jnp.sum(lam_q[2:3, :] * lam_q[3:4, :], axis=1, keepdims=True)
    lam = jnp.exp(s1) - jnp.exp(s2) + lam_init
    o0 = acc_ref[0] * _tile_lanes(1.0 / l_ref[0], dv // LANES)
    o1 = acc_ref[1] * _tile_lanes(1.0 / l_ref[1], dv // LANES)
    o = o0 - lam * o1
    o_ref[0] = (_rms(o) * gain_ref[...]).astype(BF16)


def _attn_a(p3, lam_vecs, gain_eff, n_heads, lam_init):
    b, s, n = p3.shape
    d = n // 3
    tq = min(512, s)
    tk = min(512, s)
    hd = HEAD_DIM
    kcol = d // hd
    vcol = 2 * d // (2 * hd)
    return pl.pallas_call(
        functools.partial(_attn_a_kernel, tq=tq, tk=tk, lam_init=lam_init),
        grid=(b, n_heads, s // tq),
        in_specs=[
            pl.BlockSpec((1, tq, hd), lambda bi, h, i: (bi, i, 2 * h)),
            pl.BlockSpec((1, tq, hd), lambda bi, h, i: (bi, i, 2 * h + 1)),
            pl.BlockSpec((1, s, hd), lambda bi, h, i: (bi, 0, kcol + 2 * h)),
            pl.BlockSpec((1, s, hd), lambda bi, h, i: (bi, 0, kcol + 2 * h + 1)),
            pl.BlockSpec((1, s, 2 * hd), lambda bi, h, i: (bi, 0, vcol + h)),
            pl.BlockSpec((4, hd), lambda bi, h, i: (0, 0)),
            pl.BlockSpec((1, 2 * hd), lambda bi, h, i: (0, 0)),
        ],
        out_specs=pl.BlockSpec((1, tq, 2 * hd), lambda bi, h, i: (bi, i, h)),
        out_shape=jax.ShapeDtypeStruct((b, s, d), BF16),
        scratch_shapes=[
            pltpu.VMEM((2, tq, LANES), F32),
            pltpu.VMEM((2, tq, LANES), F32),
            pltpu.VMEM((2, tq, 2 * hd), F32),
        ],
        compiler_params=_cparams(("arbitrary", "arbitrary", "arbitrary")),
        name="attn_a",
    )(p3, p3, p3, p3, p3, lam_vecs, gain_eff)


def _attn_b_kernel(q_ref, qi_ref, wi_ref, k_ref, ki_ref, v_ref, o_ref,
                   keys_ref, thr_ref, qall_ref, wib_ref, m_ref, acc_ref,
                   *, tq, tk, n_heads, group, topk, idx_bits):
    i = pl.program_id(1)
    nkc = ((i + 1) * tq + tk - 1) // tk
    nrep = tk // LANES
    kf = float(topk)

    for h in range(n_heads):
        qall_ref[h * tq:(h + 1) * tq, :] = q_ref[0, :, h * HEAD_DIM:(h + 1) * HEAD_DIM]
    wi = wi_ref[0]
    for h in range(IDX_HEADS):
        wib_ref[h] = jnp.broadcast_to(wi[:, h:h + 1], (tq, LANES))

    def score_chunk(c, carry):
        off = pl.multiple_of(c * tk, tk)
        kic = ki_ref[0, pl.ds(off, tk), :]
        sc = jnp.zeros((tq, tk), F32)
        for h in range(IDX_HEADS):
            s = _dot_nt(qi_ref[0, :, h * HEAD_DIM:(h + 1) * HEAD_DIM], kic)
            sc = sc + _tile_lanes(wib_ref[h], nrep) * jnp.maximum(s, 0.0)
        bits = pltpu.bitcast(sc, jnp.int32)
        key = bits ^ ((bits >> 31) & jnp.int32(0x7FFFFFFF))
        row = i * tq + lax.broadcasted_iota(jnp.int32, (tq, tk), 0)
        col = off + lax.broadcasted_iota(jnp.int32, (tq, tk), 1)
        keys_ref[:, pl.ds(off, tk)] = jnp.where(col <= row, key, INT_MIN)
        return carry

    lax.fori_loop(0, nkc, score_chunk, 0)

    def select_slab(rs):
        def row_total(part):
            return jnp.broadcast_to(jnp.sum(part.astype(F32), axis=1, keepdims=True), (SLAB, LANES))

        def count_ge(thr):
            def cbody(c, part):
                off = pl.multiple_of(c * tk, tk)
                kk = keys_ref[rs, pl.ds(off, tk)]
                for r in range(nrep):
                    part = part + jnp.where(kk[:, r * LANES:(r + 1) * LANES] >= thr, 1, 0)
                return part
            return row_total(lax.fori_loop(0, nkc, cbody, jnp.zeros((SLAB, LANES), jnp.int32)))

        def search(it, thr):
            cand = thr + jnp.left_shift(jnp.int32(1), 31 - it)
            return jnp.where(count_ge(cand) >= kf, cand, thr)

        thr = lax.fori_loop(0, 32, search, jnp.full((SLAB, LANES), INT_MIN, jnp.int32))
        thr_ref[rs, :] = thr

        tie = jnp.where(jnp.logical_and(count_ge(thr) > kf, thr > INT_MIN), 1.0, 0.0)
        any_tie = jnp.max(tie)

        @pl.when(any_tie > 0.0)
        def _():
            need = kf - count_ge(thr + 1)

            def count_eq_before(jcut):
                def cbody(c, part):
                    off = pl.multiple_of(c * tk, tk)
                    kk = keys_ref[rs, pl.ds(off, tk)]
                    for r in range(nrep):
                        col = off + r * LANES + lax.broadcasted_iota(jnp.int32, (SLAB, LANES), 1)
                        hit = jnp.logical_and(kk[:, r * LANES:(r + 1) * LANES] == thr, col < jcut)
                        part = part + jnp.where(hit, 1, 0)
                    return part
                return row_total(lax.fori_loop(0, nkc, cbody, jnp.zeros((SLAB, LANES), jnp.int32)))

            def jsearch(it, jcut):
                cand = jcut + jnp.left_shift(jnp.int32(1), idx_bits - 1 - it)
                return jnp.where(count_eq_before(cand) <= need, cand, jcut)

            jcut = lax.fori_loop(0, idx_bits, jsearch, jnp.zeros((SLAB, LANES), jnp.int32))

            def demote(c, carry):
                off = pl.multiple_of(c * tk, tk)
                for r in range(nrep):
                    cs = pl.ds(off + r * LANES, LANES)
                    kk = keys_ref[rs, cs]
                    col = off + r * LANES + lax.broadcasted_iota(jnp.int32, (SLAB, LANES), 1)
                    drop = jnp.logical_and(jnp.logical_and(kk == thr, col >= jcut), tie > 0.0)
                    keys_ref[rs, cs] = jnp.where(drop, kk - 1, kk)
                return carry

            lax.fori_loop(0, nkc, demote, 0)

    for r0 in range(0, tq, SLAB):
        select_slab(slice(r0, r0 + SLAB))

    thr_sel = _tile_lanes(jnp.maximum(thr_ref[...], INT_MIN + 1), nrep)
    m_ref[...] = jnp.full_like(m_ref, NEG)
    acc_ref[...] = jnp.zeros_like(acc_ref)
    rows = group * tq

    def attend(c, carry):
        off = pl.multiple_of(c * tk, tk)
        kc = k_ref[0, pl.ds(off, tk), :]
        vc = v_ref[0, pl.ds(off, tk), :]
        bias = jnp.where(keys_ref[:, pl.ds(off, tk)] >= thr_sel, 0.0, NEG)
        for g in range(n_heads // group):
            rs = slice(g * rows, (g + 1) * rows)
            s = _dot_nt(qall_ref[rs, :], kc)
            s = (s.reshape(group, tq, tk) + bias[None]).reshape(rows, tk)
            m_prev = m_ref[rs, :]
            m_new = jnp.maximum(m_prev, jnp.max(s, axis=1, keepdims=True))
            alpha = jnp.exp2(m_prev - m_new)
            p = jnp.exp2(s - _tile_lanes(m_new, nrep))
            acc_ref[rs, :] = _tile_lanes(alpha, 2) * acc_ref[rs, :] + _dot(p.astype(BF16), vc)
            m_ref[rs, :] = m_new
        return carry

    lax.fori_loop(0, nkc, attend, 0)

    for h in range(n_heads):
        hs = slice(h * tq, (h + 1) * tq)
        o_ref[0, :, h * HEAD_DIM:(h + 1) * HEAD_DIM] = (
            acc_ref[hs, 0:HEAD_DIM] * (1.0 / acc_ref[hs, HEAD_DIM:2 * HEAD_DIM])).astype(BF16)


def _attn_b(p3, wi3, n_heads, topk):
    b, s, n = p3.shape
    tq = min(256, s)
    tk = min(512, s)
    d = n_heads * HEAD_DIM
    qi_w = IDX_HEADS * HEAD_DIM
    assert qi_w % d == 0
    kcol = (d + qi_w) // HEAD_DIM
    assert kcol % 2 == 0
    group = min(4, n_heads)
    return pl.pallas_call(
        functools.partial(_attn_b_kernel, tq=tq, tk=tk, n_heads=n_heads, group=group, topk=topk,
                          idx_bits=int(s).bit_length()),
        grid=(b, s // tq),
        in_specs=[
            pl.BlockSpec((1, tq, d), lambda bi, i: (bi, i, qi_w // d)),
            pl.BlockSpec((1, tq, qi_w), lambda bi, i: (bi, i, 0)),
            pl.BlockSpec((1, tq, LANES), lambda bi, i: (bi, i, 0)),
            pl.BlockSpec((1, s, HEAD_DIM), lambda bi, i: (bi, 0, kcol)),
            pl.BlockSpec((1, s, HEAD_DIM), lambda bi, i: (bi, 0, kcol + 1)),
            pl.BlockSpec((1, s, 2 * HEAD_DIM), lambda bi, i: (bi, 0, kcol // 2 + 1)),
        ],
        out_specs=pl.BlockSpec((1, tq, d), lambda bi, i: (bi, i, 0)),
        out_shape=jax.ShapeDtypeStruct((b, s, d), BF16),
        scratch_shapes=[
            pltpu.VMEM((tq, s), jnp.int32),
            pltpu.VMEM((tq, LANES), jnp.int32),
            pltpu.VMEM((n_heads * tq, HEAD_DIM), BF16),
            pltpu.VMEM((IDX_HEADS, tq, LANES), F32),
            pltpu.VMEM((n_heads * tq, LANES), F32),
            pltpu.VMEM((n_heads * tq, 2 * HEAD_DIM), F32),
        ],
        compiler_params=_cparams(("arbitrary", "arbitrary")),
        name="attn_b",
    )(p3, p3, wi3, p3, p3, p3)


def _rope_tables(seq):
    pos = jnp.arange(seq, dtype=F32)[:, None]

    def cs(dim):
        inv = 1.0 / (ROPE_THETA ** (jnp.arange(0, dim, 2, dtype=F32) / dim))
        ang = pos * inv[None, :]
        return jnp.cos(ang), jnp.sin(ang)

    c, s = cs(HEAD_DIM)
    cos = jnp.concatenate([c, c], axis=1)
    sin = jnp.concatenate([-s, s], axis=1)
    c, s = cs(IDX_ROPE_DIM)
    one = jnp.ones((seq, LANES - IDX_ROPE_DIM), F32)
    z32 = jnp.zeros_like(s)
    z64 = jnp.zeros_like(one)
    ci = jnp.concatenate([c, c, one], axis=1)
    sa = jnp.concatenate([-s, z32, z64], axis=1)
    sb = jnp.concatenate([z32, s, z64], axis=1)
    return cos, sin, ci, sa, sb


def kernel(x, c, ada_w, ada_b, ffn_w_gate, ffn_w_up, ffn_w_down, a_w_in, a_w_out, a_q_gain, a_k_gain,
           a_lambda_q1, a_lambda_k1, a_lambda_q2, a_lambda_k2, a_subln_gain, b_w_in, b_w_out, b_q_gain,
           b_k_gain, b_kidx_gain, b_kidx_bias):
    b, s, d = x.shape
    depth = ada_w.shape[0]
    hd = HEAD_DIM
    a_heads = d // (2 * hd)
    b_heads = d // hd
    qscale = (hd ** -0.5) * math.log2(math.e)
    topk = min(TOPK_MAX, s // 4)
    tn = 512

    cos, sin, ci, sa, sb = _rope_tables(s)
    mod = _ada_mod(c, ada_w, ada_b)
    wg = ffn_w_gate.astype(BF16)
    wu = ffn_w_up.astype(BF16)
    wd = ffn_w_down.astype(BF16)

    x2 = x.reshape(b * s, d)
    for i in range(depth):
        j = i // 2
        x2 = _ffn(x2, mod[i], wg[i, 0], wu[i, 0], wd[i, 0], 0, s)
        if i % 2 == 0:
            lam_init = 0.8 - 0.6 * math.exp(-0.3 * i)
            reps = tn // hd
            gains = jnp.concatenate([
                jnp.tile(jnp.tile(a_q_gain[j] * qscale, reps)[None], (d // tn, 1)),
                jnp.tile(jnp.tile(a_k_gain[j], reps)[None], (d // tn, 1)),
                jnp.ones((d // tn, tn), F32),
            ], axis=0)[:, None, :]
            p = _proj_a(x2, mod[i], a_w_in[j].astype(BF16), gains, cos, sin, s)
            lam_vecs = jnp.stack([a_lambda_q1[j], a_lambda_k1[j], a_lambda_q2[j], a_lambda_k2[j]])
            gain_eff = (a_subln_gain[j] * (1.0 - lam_init))[None, :]
            o = _attn_a(p.reshape(b, s, 3 * d), lam_vecs, gain_eff, a_heads, lam_init)
            x2 = _out_proj(o.reshape(b * s, d), a_w_out[j].astype(BF16), x2, mod[i], s)
        else:
            w = b_w_in[j]
            q_w = b_heads * hd
            qi_w = IDX_HEADS * hd
            o_k, o_v, o_qi, o_ki, o_wi = q_w, q_w + hd, q_w + 2 * hd, q_w + 2 * hd + qi_w, q_w + 3 * hd + qi_w
            w_re = jnp.concatenate([
                w[:, o_qi:o_qi + qi_w], w[:, :q_w], w[:, o_k:o_k + hd], w[:, o_ki:o_ki + hd],
                w[:, o_v:o_v + hd], w[:, o_wi:o_wi + IDX_HEADS],
                jnp.zeros((d, hd - IDX_HEADS), F32),
            ], axis=1).astype(BF16)
            n_tiles = w_re.shape[1] // tn
            reps = tn // hd
            g0 = jnp.zeros((n_tiles, 2, tn), F32)
            g0 = g0.at[qi_w // tn:(qi_w + q_w) // tn, 0, :].set(jnp.tile(b_q_gain[j] * qscale, reps)[None])
            g0 = g0.at[n_tiles - 1, 0, 0:hd].set(b_k_gain[j])
            g0 = g0.at[n_tiles - 1, 0, hd:2 * hd].set(b_kidx_gain[j])
            g0 = g0.at[n_tiles - 1, 1, hd:2 * hd].set(b_kidx_bias[j])
            p, wi = _proj_b(x2, mod[i], w_re, g0, cos, sin, ci, sa, sb, s)
            o = _attn_b(p.reshape(b, s, w_re.shape[1]), wi.reshape(b, s, LANES), b_heads, topk)
            x2 = _out_proj(o.reshape(b * s, d), b_w_out[j].astype(BF16), x2, mod[i], s)
        x2 = _ffn(x2, mod[i], wg[i, 1], wu[i, 1], wd[i, 1], 6, s)
    return x2.reshape(b, s, d)
```

```python
import functools
import math

import jax
import jax.numpy as jnp
import numpy as np
from jax import lax
from jax.experimental import pallas as pl
from jax.experimental.pallas import tpu as pltpu

ROPE_THETA = 10000.0
NORM_EPS = 1e-6
N_ADA = 9
HEAD_DIM = 128
IDX_HEADS = 16
IDX_ROPE_DIM = 64
TOPK_MAX = 256
LANES = 128
ACC_CHAINS = 64
V7X_VMEM_LIMIT = 56 * 1024 * 1024
NEG = -1e30
INT_MIN = np.int32(-(2 ** 31))

BF16 = jnp.bfloat16
F32 = jnp.float32


def _cparams(sem):
    return pltpu.CompilerParams(dimension_semantics=sem, vmem_limit_bytes=V7X_VMEM_LIMIT)


def _dot(a, b):
    return jnp.dot(a, b, preferred_element_type=F32)


def _dot_nt(a, b):
    return lax.dot_general(a, b, (((1,), (1,)), ((), ())), preferred_element_type=F32)


def _tile_lanes(x, n):
    return x if n == 1 else jnp.concatenate([x] * n, axis=1)


def _modulate(x, shift, scale):
    ms = jnp.mean(x * x, axis=-1, keepdims=True)
    return x * lax.rsqrt(ms + NORM_EPS) * (1.0 + scale) + shift


def _rms(x):
    return x * lax.rsqrt(jnp.mean(x * x, axis=-1, keepdims=True) + NORM_EPS)


def _ada_kernel(c_ref, w_ref, b_ref, o_ref):
    c = c_ref[...]
    ca = (c * (1.0 / (1.0 + jnp.exp(-c)))).astype(BF16)
    o_ref[0] = _dot(ca, w_ref[0].astype(BF16)) + b_ref[0]


def _ada_mod(c, ada_w, ada_b):
    depth, d, n = ada_w.shape
    b = c.shape[0]
    rows = 8
    c8 = jnp.zeros((rows, d), F32).at[:b].set(c)
    tn = math.gcd(n, 1024)
    out = pl.pallas_call(
        _ada_kernel,
        grid=(depth, n // tn),
        in_specs=[
            pl.BlockSpec((rows, d), lambda i, j: (0, 0)),
            pl.BlockSpec((1, d, tn), lambda i, j: (i, 0, j)),
            pl.BlockSpec((1, 1, tn), lambda i, j: (i, 0, j)),
        ],
        out_specs=pl.BlockSpec((1, rows, tn), lambda i, j: (i, 0, j)),
        out_shape=jax.ShapeDtypeStruct((depth, rows, n), F32),
        compiler_params=_cparams(("arbitrary", "arbitrary")),
        name="ada_mod",
    )(c8, ada_w, ada_b.reshape(depth, 1, n))
    return out[:, :b].reshape(depth, b, N_ADA, d)


def _ffn_kernel(x_ref, mod_ref, wg_ref, wu_ref, wd_ref, o_ref, h_ref, *, row0, nf):
    f = pl.program_id(1)

    @pl.when(f == 0)
    def _():
        h = _modulate(x_ref[...], mod_ref[0, row0:row0 + 1, :], mod_ref[0, row0 + 1:row0 + 2, :])
        h_ref[...] = h.astype(BF16)
        o_ref[...] = jnp.zeros_like(o_ref)

    h = h_ref[...]
    a = _dot(h, wg_ref[...])
    u = _dot(h, wu_ref[...])
    act = (a * (1.0 / (1.0 + jnp.exp(-a))) * u).astype(BF16)
    o_ref[...] += _dot(act, wd_ref[...])

    @pl.when(f == nf - 1)
    def _():
        g = mod_ref[0, row0 + 2:row0 + 3, :]
        o_ref[...] = x_ref[...] + 0.5 * g * o_ref[...]


def _ffn(x2, mod, wg, wu, wd, row0, seq):
    t, d = x2.shape
    f_dim = wg.shape[1]
    tm = min(512, seq)
    tf = 512 if f_dim % 512 == 0 else f_dim
    nsb = seq // tm
    nf = f_dim // tf
    return pl.pallas_call(
        functools.partial(_ffn_kernel, row0=row0, nf=nf),
        grid=(t // tm, nf),
        in_specs=[
            pl.BlockSpec((tm, d), lambda i, f: (i, 0)),
            pl.BlockSpec((1, N_ADA, d), lambda i, f: (i // nsb, 0, 0)),
            pl.BlockSpec((d, tf), lambda i, f: (0, f)),
            pl.BlockSpec((d, tf), lambda i, f: (0, f)),
            pl.BlockSpec((tf, d), lambda i, f: (f, 0)),
        ],
        out_specs=pl.BlockSpec((tm, d), lambda i, f: (i, 0)),
        out_shape=jax.ShapeDtypeStruct((t, d), F32),
        scratch_shapes=[pltpu.VMEM((tm, d), BF16)],
        compiler_params=_cparams(("arbitrary", "arbitrary")),
        name="ffn",
    )(x2, mod, wg, wu, wd)


def _rope_full(r, cos, sin):
    return r * cos + pltpu.roll(r, HEAD_DIM // 2, 1) * sin


def _rope_partial(r, ci, sa, sb):
    q = IDX_ROPE_DIM // 2
    return r * ci + pltpu.roll(r, LANES - q, 1) * sa + pltpu.roll(r, q, 1) * sb


def _proj_a_kernel(x_ref, mod_ref, w_ref, gain_ref, cos_ref, sin_ref, o_ref, h_ref, *, n_qk, tn):
    j = pl.program_id(1)

    @pl.when(j == 0)
    def _():
        h = _modulate(x_ref[...], mod_ref[0, 3:4, :], mod_ref[0, 4:5, :])
        h_ref[...] = h.astype(BF16)

    p = _dot(h_ref[...], w_ref[...])

    @pl.when(j < n_qk)
    def _():
        cos = cos_ref[...]
        sin = sin_ref[...]
        for c in range(tn // LANES):
            sl = slice(c * LANES, (c + 1) * LANES)
            r = _rms(p[:, sl]) * gain_ref[0, :, sl]
            o_ref[:, sl] = _rope_full(r, cos, sin).astype(BF16)

    @pl.when(j >= n_qk)
    def _():
        o_ref[...] = p.astype(BF16)


def _proj_a(x2, mod, w, gains, cos, sin, seq):
    t, d = x2.shape
    n = w.shape[1]
    tm = min(512, seq)
    tn = 512
    nsb = seq // tm
    n_qk = 2 * d // tn
    return pl.pallas_call(
        functools.partial(_proj_a_kernel, n_qk=n_qk, tn=tn),
        grid=(t // tm, n // tn),
        in_specs=[
            pl.BlockSpec((tm, d), lambda i, j: (i, 0)),
            pl.BlockSpec((1, N_ADA, d), lambda i, j: (i // nsb, 0, 0)),
            pl.BlockSpec((d, tn), lambda i, j: (0, j)),
            pl.BlockSpec((1, 1, tn), lambda i, j: (j, 0, 0)),
            pl.BlockSpec((tm, LANES), lambda i, j: (i % nsb, 0)),
            pl.BlockSpec((tm, LANES), lambda i, j: (i % nsb, 0)),
        ],
        out_specs=pl.BlockSpec((tm, tn), lambda i, j: (i, j)),
        out_shape=jax.ShapeDtypeStruct((t, n), BF16),
        scratch_shapes=[pltpu.VMEM((tm, d), BF16)],
        compiler_params=_cparams(("arbitrary", "arbitrary")),
        name="proj_a",
    )(x2, mod, w, gains, cos, sin)


def _proj_b_kernel(x_ref, mod_ref, w_ref, gain_ref, cos_ref, sin_ref, ci_ref, sa_ref, sb_ref,
                   o_ref, wi_ref, h_ref, *, n_q, n_qi, tn, wi_scale):
    j = pl.program_id(1)

    @pl.when(j == 0)
    def _():
        h = _modulate(x_ref[...], mod_ref[0, 3:4, :], mod_ref[0, 4:5, :])
        h_ref[...] = h.astype(BF16)

    p = _dot(h_ref[...], w_ref[...])

    @pl.when(j < n_qi)
    def _():
        ci = ci_ref[...]
        sa = sa_ref[...]
        sb = sb_ref[...]
        for c in range(tn // LANES):
            sl = slice(c * LANES, (c + 1) * LANES)
            o_ref[:, sl] = _rope_partial(p[:, sl], ci, sa, sb).astype(BF16)

    @pl.when(jnp.logical_and(j >= n_qi, j < n_q + n_qi))
    def _():
        cos = cos_ref[...]
        sin = sin_ref[...]
        for c in range(tn // LANES):
            sl = slice(c * LANES, (c + 1) * LANES)
            r = _rms(p[:, sl]) * gain_ref[0, 0:1, sl]
            o_ref[:, sl] = _rope_full(r, cos, sin).astype(BF16)

    @pl.when(j == n_q + n_qi)
    def _():
        k = _rms(p[:, 0:LANES]) * gain_ref[0, 0:1, 0:LANES]
        o_ref[:, 0:LANES] = _rope_full(k, cos_ref[...], sin_ref[...]).astype(BF16)
        ki = p[:, LANES:2 * LANES]
        mu = jnp.mean(ki, axis=-1, keepdims=True)
        kc = ki - mu
        var = jnp.mean(kc * kc, axis=-1, keepdims=True)
        kn = kc * lax.rsqrt(var + NORM_EPS) * gain_ref[0, 0:1, LANES:2 * LANES] \
            + gain_ref[0, 1:2, LANES:2 * LANES]
        o_ref[:, LANES:2 * LANES] = _rope_partial(kn, ci_ref[...], sa_ref[...], sb_ref[...]).astype(BF16)
        o_ref[:, 2 * LANES:3 * LANES] = p[:, 2 * LANES:3 * LANES].astype(BF16)
        wi = p[:, 3 * LANES:4 * LANES] * wi_scale
        o_ref[:, 3 * LANES:4 * LANES] = wi.astype(BF16)
        wi_ref[...] = wi


def _proj_b(x2, mod, w, gains, cos, sin, ci, sa, sb, seq):
    t, d = x2.shape
    n = w.shape[1]
    tm = min(512, seq)
    tn = 512
    nsb = seq // tm
    n_q = d // tn
    n_qi = IDX_HEADS * HEAD_DIM // tn
    assert n == (n_q + n_qi + 1) * tn
    wi_scale = (IDX_HEADS ** -0.5) * (HEAD_DIM ** -0.5)
    tab = pl.BlockSpec((tm, LANES), lambda i, j: (i % nsb, 0))
    return pl.pallas_call(
        functools.partial(_proj_b_kernel, n_q=n_q, n_qi=n_qi, tn=tn, wi_scale=wi_scale),
        grid=(t // tm, n // tn),
        in_specs=[
            pl.BlockSpec((tm, d), lambda i, j: (i, 0)),
            pl.BlockSpec((1, N_ADA, d), lambda i, j: (i // nsb, 0, 0)),
            pl.BlockSpec((d, tn), lambda i, j: (0, j)),
            pl.BlockSpec((1, 2, tn), lambda i, j: (j, 0, 0)),
            tab, tab, tab, tab, tab,
        ],
        out_specs=[
            pl.BlockSpec((tm, tn), lambda i, j: (i, j)),
            pl.BlockSpec((tm, LANES), lambda i, j: (i, 0)),
        ],
        out_shape=[
            jax.ShapeDtypeStruct((t, n), BF16),
            jax.ShapeDtypeStruct((t, LANES), F32),
        ],
        scratch_shapes=[pltpu.VMEM((tm, d), BF16)],
        compiler_params=_cparams(("arbitrary", "arbitrary")),
        name="proj_b",
    )(x2, mod, w, gains, cos, sin, ci, sa, sb)


def _out_kernel(o_ref, w_ref, x_ref, mod_ref, y_ref):
    y = _dot(o_ref[...], w_ref[...])
    y_ref[...] = x_ref[...] + mod_ref[0, 5:6, :] * y


def _out_proj(o2, w, x2, mod, seq):
    t, d = x2.shape
    k = o2.shape[1]
    tm = min(512, seq)
    nsb = seq // tm
    return pl.pallas_call(
        _out_kernel,
        grid=(t // tm,),
        in_specs=[
            pl.BlockSpec((tm, k), lambda i: (i, 0)),
            pl.BlockSpec((k, d), lambda i: (0, 0)),
            pl.BlockSpec((tm, d), lambda i: (i, 0)),
            pl.BlockSpec((1, N_ADA, d), lambda i: (i // nsb, 0, 0)),
        ],
        out_specs=pl.BlockSpec((tm, d), lambda i: (i, 0)),
        out_shape=jax.ShapeDtypeStruct((t, d), F32),
        compiler_params=_cparams(("arbitrary",)),
        name="out_proj",
    )(o2, w, x2, mod)


def _softmax_step(s, m_ref, l_ref, acc_ref, v, idx):
    tk = s.shape[1]
    dv = v.shape[1]
    m_prev = m_ref[idx]
    m_new = jnp.maximum(m_prev, jnp.max(s, axis=1, keepdims=True))
    alpha = jnp.exp2(m_prev - m_new)
    p = jnp.exp2(s - _tile_lanes(m_new, tk // LANES))
    l_ref[idx] = alpha * l_ref[idx] + jnp.sum(p, axis=1, keepdims=True)
    acc_ref[idx] = _tile_lanes(alpha, dv // LANES) * acc_ref[idx] + _dot(p.astype(BF16), v)
    m_ref[idx] = m_new


def _attn_a_kernel(q0_ref, q1_ref, k0_ref, k1_ref, v_ref, lam_ref, gain_ref, o_ref,
                   m_ref, l_ref, acc_ref, *, tq, tk, lam_init):
    qi = pl.program_id(2)
    m_ref[...] = jnp.full_like(m_ref, NEG)
    l_ref[...] = jnp.zeros_like(l_ref)
    acc_ref[...] = jnp.zeros_like(acc_ref)
    q0 = q0_ref[0]
    q1 = q1_ref[0]
    n_full = (qi * tq) // tk

    def chunk(c, masked):
        off = pl.multiple_of(c * tk, tk)
        v = v_ref[0, pl.ds(off, tk), :]
        s0 = _dot_nt(q0, k0_ref[0, pl.ds(off, tk), :])
        s1 = _dot_nt(q1, k1_ref[0, pl.ds(off, tk), :])
        if masked:
            row = qi * tq + lax.broadcasted_iota(jnp.int32, (tq, tk), 0)
            col = off + lax.broadcasted_iota(jnp.int32, (tq, tk), 1)
            keep = col <= row
            s0 = jnp.where(keep, s0, NEG)
            s1 = jnp.where(keep, s1, NEG)
        _softmax_step(s0, m_ref, l_ref, acc_ref, v, 0)
        _softmax_step(s1, m_ref, l_ref, acc_ref, v, 1)

    def body(c, carry):
        chunk(c, False)
        return carry

    lax.fori_loop(0, n_full, body, 0)
    chunk(n_full, True)

    dv = acc_ref.shape[2]
    lam_q = lam_ref[...]
    s1 = jnp.sum(lam_q[0:1, :] * lam_q[1:2, :], axis=1, keepdims=True)
    s2 = jnp.sum(lam_q[2:3, :] * lam_q[3:4, :], axis=1, keepdims=True)
    lam = jnp.exp(s1) - jnp.exp(s2) + lam_init
    o0 = acc_ref[0] * _tile_lanes(1.0 / l_ref[0], dv // LANES)
    o1 = acc_ref[1] * _tile_lanes(1.0 / l_ref[1], dv // LANES)
    o = o0 - lam * o1
    o_ref[0] = (_rms(o) * gain_ref[...]).astype(BF16)


def _attn_a(p3, lam_vecs, gain_eff, n_heads, lam_init):
    b, s, n = p3.shape
    d = n // 3
    tq = min(512, s)
    tk = min(512, s)
    hd = HEAD_DIM
    kcol = d // hd
    vcol = 2 * d // (2 * hd)
    return pl.pallas_call(
        functools.partial(_attn_a_kernel, tq=tq, tk=tk, lam_init=lam_init),
        grid=(b, n_heads, s // tq),
        in_specs=[
            pl.BlockSpec((1, tq, hd), lambda bi, h, i: (bi, i, 2 * h)),
            pl.BlockSpec((1, tq, hd), lambda bi, h, i: (bi, i, 2 * h + 1)),
            pl.BlockSpec((1, s, hd), lambda bi, h, i: (bi, 0, kcol + 2 * h)),
            pl.BlockSpec((1, s, hd), lambda bi, h, i: (bi, 0, kcol + 2 * h + 1)),
            pl.BlockSpec((1, s, 2 * hd), lambda bi, h, i: (bi, 0, vcol + h)),
            pl.BlockSpec((4, hd), lambda bi, h, i: (0, 0)),
            pl.BlockSpec((1, 2 * hd), lambda bi, h, i: (0, 0)),
        ],
        out_specs=pl.BlockSpec((1, tq, 2 * hd), lambda bi, h, i: (bi, i, h)),
        out_shape=jax.ShapeDtypeStruct((b, s, d), BF16),
        scratch_shapes=[
            pltpu.VMEM((2, tq, LANES), F32),
            pltpu.VMEM((2, tq, LANES), F32),
            pltpu.VMEM((2, tq, 2 * hd), F32),
        ],
        compiler_params=_cparams(("arbitrary", "arbitrary", "arbitrary")),
        name="attn_a",
    )(p3, p3, p3, p3, p3, lam_vecs, gain_eff)


def _attn_b_kernel(q_ref, qi_ref, wi_ref, k_ref, ki_ref, vt_ref, o_ref,
                   keys_ref, m_ref, acc_ref, *, tq, tk, n_heads, topk, idx_bits):
    i = pl.program_id(1)
    nkc = ((i + 1) * tq + tk - 1) // tk
    kf = float(topk)

    def key_pos(off, rows):
        return off + lax.broadcasted_iota(jnp.int32, (rows, tq), 0)

    def score_chunk(c, carry):
        off = pl.multiple_of(c * tk, tk)
        kic = ki_ref[0, pl.ds(off, tk), :]
        sc = jnp.zeros((tk, tq), F32)
        for h in range(IDX_HEADS):
            s = _dot_nt(kic, qi_ref[0, :, h * HEAD_DIM:(h + 1) * HEAD_DIM])
            sc = sc + wi_ref[0, h:h + 1, :] * jnp.maximum(s, 0.0)
        bits = pltpu.bitcast(sc, jnp.int32)
        key = bits ^ ((bits >> 31) & jnp.int32(0x7FFFFFFF))
        qpos = i * tq + lax.broadcasted_iota(jnp.int32, (tk, tq), 1)
        keys_ref[pl.ds(off, tk), :] = jnp.where(key_pos(off, tk) <= qpos, key, INT_MIN)
        return carry

    lax.fori_loop(0, nkc, score_chunk, 0)

    def col_total(part):
        return jnp.sum(part.astype(F32), axis=0, keepdims=True)

    def fold_chunk(off, part, hit_fn):
        for r in range(tk // ACC_CHAINS):
            ro = off + r * ACC_CHAINS
            part = part + jnp.where(hit_fn(keys_ref[pl.ds(ro, ACC_CHAINS), :], ro), 1, 0)
        return part

    def count_ge(thr):
        def cbody(c, part):
            off = pl.multiple_of(c * tk, tk)
            return fold_chunk(off, part, lambda kk, ro: kk >= thr)
        return col_total(lax.fori_loop(0, nkc, cbody, jnp.zeros((ACC_CHAINS, tq), jnp.int32)))

    def search(it, thr):
        cand = thr + jnp.left_shift(jnp.int32(1), 31 - it)
        return jnp.where(count_ge(cand) >= kf, cand, thr)

    thr = lax.fori_loop(0, 32, search, jnp.full((1, tq), INT_MIN, jnp.int32))

    tie = jnp.where(jnp.logical_and(count_ge(thr) > kf, thr > INT_MIN), 1.0, 0.0)
    any_tie = jnp.max(tie)

    @pl.when(any_tie > 0.0)
    def _():
        need = kf - count_ge(thr + 1)

        def count_eq_before(jcut):
            def cbody(c, part):
                off = pl.multiple_of(c * tk, tk)
                return fold_chunk(off, part, lambda kk, ro: jnp.logical_and(
                    kk == thr, key_pos(ro, ACC_CHAINS) < jcut))
            return col_total(lax.fori_loop(0, nkc, cbody, jnp.zeros((ACC_CHAINS, tq), jnp.int32)))

        def jsearch(it, jcut):
            cand = jcut + jnp.left_shift(jnp.int32(1), idx_bits - 1 - it)
            return jnp.where(count_eq_before(cand) <= need, cand, jcut)

        jcut = lax.fori_loop(0, idx_bits, jsearch, jnp.zeros((1, tq), jnp.int32))

        def demote(c, carry):
            off = pl.multiple_of(c * tk, tk)
            kk = keys_ref[pl.ds(off, tk), :]
            drop = jnp.logical_and(jnp.logical_and(kk == thr, key_pos(off, tk) >= jcut), tie > 0.0)
            keys_ref[pl.ds(off, tk), :] = jnp.where(drop, kk - 1, kk)
            return carry

        lax.fori_loop(0, nkc, demote, 0)

    thr_sel = jnp.maximum(thr, INT_MIN + 1)
    m_ref[...] = jnp.full_like(m_ref, NEG)
    acc_ref[...] = jnp.zeros_like(acc_ref)

    def attend(c, carry):
        off = pl.multiple_of(c * tk, tk)
        kc = k_ref[0, pl.ds(off, tk), :]
        vt = vt_ref[0, :, pl.ds(off, tk)]
        bias = jnp.where(keys_ref[pl.ds(off, tk), :] >= thr_sel, 0.0, NEG)
        for h in range(n_heads):
            s = _dot_nt(kc, q_ref[0, :, h * HEAD_DIM:(h + 1) * HEAD_DIM]) + bias
            m_prev = m_ref[h]
            m_new = jnp.maximum(m_prev, jnp.max(s, axis=0, keepdims=True))
            alpha = jnp.exp2(m_prev - m_new)
            p = jnp.exp2(s - m_new)
            acc_ref[h] = alpha * acc_ref[h] + _dot(vt, p.astype(BF16))
            m_ref[h] = m_new
        return carry

    lax.fori_loop(0, nkc, attend, 0)

    for h in range(n_heads):
        a = acc_ref[h]
        o = a[0:HEAD_DIM, :] * (1.0 / a[HEAD_DIM:HEAD_DIM + 1, :])
        o_ref[0, :, h * HEAD_DIM:(h + 1) * HEAD_DIM] = o.T.astype(BF16)


def _attn_b(p3, wi_t, v1t, n_heads, topk):
    b, s, n = p3.shape
    tq = min(256, s)
    tk = min(512, s)
    d = n_heads * HEAD_DIM
    qi_w = IDX_HEADS * HEAD_DIM
    assert qi_w % d == 0
    kcol = (d + qi_w) // HEAD_DIM
    vrows = v1t.shape[1]
    assert vrows == 2 * HEAD_DIM
    return pl.pallas_call(
        functools.partial(_attn_b_kernel, tq=tq, tk=tk, n_heads=n_heads, topk=topk,
                          idx_bits=int(s).bit_length()),
        grid=(b, s // tq),
        in_specs=[
            pl.BlockSpec((1, tq, d), lambda bi, i: (bi, i, qi_w // d)),
            pl.BlockSpec((1, tq, qi_w), lambda bi, i: (bi, i, 0)),
            pl.BlockSpec((1, LANES, tq), lambda bi, i: (bi, 0, i)),
            pl.BlockSpec((1, s, HEAD_DIM), lambda bi, i: (bi, 0, kcol)),
            pl.BlockSpec((1, s, HEAD_DIM), lambda bi, i: (bi, 0, kcol + 1)),
            pl.BlockSpec((1, vrows, s), lambda bi, i: (bi, 0, 0)),
        ],
        out_specs=pl.BlockSpec((1, tq, d), lambda bi, i: (bi, i, 0)),
        out_shape=jax.ShapeDtypeStruct((b, s, d), BF16),
        scratch_shapes=[
            pltpu.VMEM((s, tq), jnp.int32),
            pltpu.VMEM((n_heads, 1, tq), F32),
            pltpu.VMEM((n_heads, vrows, tq), F32),
        ],
        compiler_params=_cparams(("arbitrary", "arbitrary")),
        name="attn_b",
    )(p3, p3, wi_t, p3, p3, v1t)


def _rope_tables(seq):
    pos = jnp.arange(seq, dtype=F32)[:, None]

    def cs(dim):
        inv = 1.0 / (ROPE_THETA ** (jnp.arange(0, dim, 2, dtype=F32) / dim))
        ang = pos * inv[None, :]
        return jnp.cos(ang), jnp.sin(ang)

    c, s = cs(HEAD_DIM)
    cos = jnp.concatenate([c, c], axis=1)
    sin = jnp.concatenate([-s, s], axis=1)
    c, s = cs(IDX_ROPE_DIM)
    one = jnp.ones((seq, LANES - IDX_ROPE_DIM), F32)
    z32 = jnp.zeros_like(s)
    z64 = jnp.zeros_like(one)
    ci = jnp.concatenate([c, c, one], axis=1)
    sa = jnp.concatenate([-s, z32, z64], axis=1)
    sb = jnp.concatenate([z32, s, z64], axis=1)
    return cos, sin, ci, sa, sb


def kernel(x, c, ada_w, ada_b, ffn_w_gate, ffn_w_up, ffn_w_down, a_w_in, a_w_out, a_q_gain, a_k_gain,
           a_lambda_q1, a_lambda_k1, a_lambda_q2, a_lambda_k2, a_subln_gain, b_w_in, b_w_out, b_q_gain,
           b_k_gain, b_kidx_gain, b_kidx_bias):
    b, s, d = x.shape
    depth = ada_w.shape[0]
    hd = HEAD_DIM
    a_heads = d // (2 * hd)
    b_heads = d // hd
    qscale = (hd ** -0.5) * math.log2(math.e)
    topk = min(TOPK_MAX, s // 4)
    tn = 512

    cos, sin, ci, sa, sb = _rope_tables(s)
    mod = _ada_mod(c, ada_w, ada_b)
    wg = ffn_w_gate.astype(BF16)
    wu = ffn_w_up.astype(BF16)
    wd = ffn_w_down.astype(BF16)

    x2 = x.reshape(b * s, d)
    for i in range(depth):
        j = i // 2
        x2 = _ffn(x2, mod[i], wg[i, 0], wu[i, 0], wd[i, 0], 0, s)
        if i % 2 == 0:
            lam_init = 0.8 - 0.6 * math.exp(-0.3 * i)
            reps = tn // hd
            gains = jnp.concatenate([
                jnp.tile(jnp.tile(a_q_gain[j] * qscale, reps)[None], (d // tn, 1)),
                jnp.tile(jnp.tile(a_k_gain[j], reps)[None], (d // tn, 1)),
                jnp.ones((d // tn, tn), F32),
            ], axis=0)[:, None, :]
            p = _proj_a(x2, mod[i], a_w_in[j].astype(BF16), gains, cos, sin, s)
            lam_vecs = jnp.stack([a_lambda_q1[j], a_lambda_k1[j], a_lambda_q2[j], a_lambda_k2[j]])
            gain_eff = (a_subln_gain[j] * (1.0 - lam_init))[None, :]
            o = _attn_a(p.reshape(b, s, 3 * d), lam_vecs, gain_eff, a_heads, lam_init)
            x2 = _out_proj(o.reshape(b * s, d), a_w_out[j].astype(BF16), x2, mod[i], s)
        else:
            w = b_w_in[j]
            q_w = b_heads * hd
            qi_w = IDX_HEADS * hd
            o_k, o_v, o_qi, o_ki, o_wi = q_w, q_w + hd, q_w + 2 * hd, q_w + 2 * hd + qi_w, q_w + 3 * hd + qi_w
            w_re = jnp.concatenate([
                w[:, o_qi:o_qi + qi_w], w[:, :q_w], w[:, o_k:o_k + hd], w[:, o_ki:o_ki + hd],
                w[:, o_v:o_v + hd], w[:, o_wi:o_wi + IDX_HEADS],
                jnp.zeros((d, hd - IDX_HEADS), F32),
            ], axis=1).astype(BF16)
            n_tiles = w_re.shape[1] // tn
            reps = tn // hd
            g0 = jnp.zeros((n_tiles, 2, tn), F32)
            g0 = g0.at[qi_w // tn:(qi_w + q_w) // tn, 0, :].set(jnp.tile(b_q_gain[j] * qscale, reps)[None])
            g0 = g0.at[n_tiles - 1, 0, 0:hd].set(b_k_gain[j])
            g0 = g0.at[n_tiles - 1, 0, hd:2 * hd].set(b_kidx_gain[j])
            g0 = g0.at[n_tiles - 1, 1, hd:2 * hd].set(b_kidx_bias[j])
            p, wi = _proj_b(x2, mod[i], w_re, g0, cos, sin, ci, sa, sb, s)
            p3 = p.reshape(b, s, w_re.shape[1])
            v_t = jnp.swapaxes(p3[:, :, qi_w + q_w + 2 * hd:qi_w + q_w + 3 * hd], 1, 2)
            v1t = jnp.concatenate([v_t, jnp.ones_like(v_t)], axis=1)
            wi_t = jnp.swapaxes(wi.reshape(b, s, LANES), 1, 2)
            o = _attn_b(p3, wi_t, v1t, b_heads, topk)
            x2 = _out_proj(o.reshape(b * s, d), b_w_out[j].astype(BF16), x2, mod[i], s)
        x2 = _ffn(x2, mod[i], wg[i, 1], wu[i, 1], wd[i, 1], 6, s)
    return x2.reshape(b, s, d)
```

```python
import functools
import math

import jax
import jax.numpy as jnp
import numpy as np
from jax import lax
from jax.experimental import pallas as pl
from jax.experimental.pallas import tpu as pltpu

ROPE_THETA = 10000.0
NORM_EPS = 1e-6
N_ADA = 9
HEAD_DIM = 128
IDX_HEADS = 16
IDX_ROPE_DIM = 64
TOPK_MAX = 256
LANES = 128
ACC_CHAINS = 64
V7X_VMEM_LIMIT = 56 * 1024 * 1024
NEG = -1e30
INT_MIN = np.int32(-(2 ** 31))

BF16 = jnp.bfloat16
F32 = jnp.float32


def _cparams(sem):
    return pltpu.CompilerParams(dimension_semantics=sem, vmem_limit_bytes=V7X_VMEM_LIMIT)


def _dot(a, b):
    return jnp.dot(a, b, preferred_element_type=F32)


def _dot_nt(a, b):
    return lax.dot_general(a, b, (((1,), (1,)), ((), ())), preferred_element_type=F32)


def _tile_lanes(x, n):
    return x if n == 1 else jnp.concatenate([x] * n, axis=1)


def _modulate(x, shift, scale):
    ms = jnp.mean(x * x, axis=-1, keepdims=True)
    return x * lax.rsqrt(ms + NORM_EPS) * (1.0 + scale) + shift


def _rms(x):
    return x * lax.rsqrt(jnp.mean(x * x, axis=-1, keepdims=True) + NORM_EPS)


def _ada_kernel(c_ref, w_ref, b_ref, o_ref):
    c = c_ref[...]
    ca = (c * (1.0 / (1.0 + jnp.exp(-c)))).astype(BF16)
    o_ref[0] = _dot(ca, w_ref[0].astype(BF16)) + b_ref[0]


def _ada_mod(c, ada_w, ada_b):
    depth, d, n = ada_w.shape
    b = c.shape[0]
    rows = 8
    c8 = jnp.zeros((rows, d), F32).at[:b].set(c)
    tn = math.gcd(n, 1024)
    out = pl.pallas_call(
        _ada_kernel,
        grid=(depth, n // tn),
        in_specs=[
            pl.BlockSpec((rows, d), lambda i, j: (0, 0)),
            pl.BlockSpec((1, d, tn), lambda i, j: (i, 0, j)),
            pl.BlockSpec((1, 1, tn), lambda i, j: (i, 0, j)),
        ],
        out_specs=pl.BlockSpec((1, rows, tn), lambda i, j: (i, 0, j)),
        out_shape=jax.ShapeDtypeStruct((depth, rows, n), F32),
        compiler_params=_cparams(("arbitrary", "arbitrary")),
        name="ada_mod",
    )(c8, ada_w, ada_b.reshape(depth, 1, n))
    return out[:, :b].reshape(depth, b, N_ADA, d)


def _ffn_kernel(x_ref, mod_ref, wg_ref, wu_ref, wd_ref, o_ref, h_ref, *, row0, nf):
    f = pl.program_id(1)

    @pl.when(f == 0)
    def _():
        h = _modulate(x_ref[...], mod_ref[0, row0:row0 + 1, :], mod_ref[0, row0 + 1:row0 + 2, :])
        h_ref[...] = h.astype(BF16)
        o_ref[...] = jnp.zeros_like(o_ref)

    h = h_ref[...]
    a = _dot(h, wg_ref[...])
    u = _dot(h, wu_ref[...])
    act = (a * (1.0 / (1.0 + jnp.exp(-a))) * u).astype(BF16)
    o_ref[...] += _dot(act, wd_ref[...])

    @pl.when(f == nf - 1)
    def _():
        g = mod_ref[0, row0 + 2:row0 + 3, :]
        o_ref[...] = x_ref[...] + 0.5 * g * o_ref[...]


def _ffn(x2, mod, wg, wu, wd, row0, seq):
    t, d = x2.shape
    f_dim = wg.shape[1]
    tm = min(1024, seq)
    tf = 512 if f_dim % 512 == 0 else f_dim
    nsb = seq // tm
    nf = f_dim // tf
    return pl.pallas_call(
        functools.partial(_ffn_kernel, row0=row0, nf=nf),
        grid=(t // tm, nf),
        in_specs=[
            pl.BlockSpec((tm, d), lambda i, f: (i, 0), pipeline_mode=pl.Buffered(1)),
            pl.BlockSpec((1, N_ADA, d), lambda i, f: (i // nsb, 0, 0)),
            pl.BlockSpec((d, tf), lambda i, f: (0, f)),
            pl.BlockSpec((d, tf), lambda i, f: (0, f)),
            pl.BlockSpec((tf, d), lambda i, f: (f, 0)),
        ],
        out_specs=pl.BlockSpec((tm, d), lambda i, f: (i, 0)),
        out_shape=jax.ShapeDtypeStruct((t, d), F32),
        scratch_shapes=[pltpu.VMEM((tm, d), BF16)],
        compiler_params=_cparams(("arbitrary", "arbitrary")),
        name="ffn",
    )(x2, mod, wg, wu, wd)


def _rope_full(r, cos, sin):
    return r * cos + pltpu.roll(r, HEAD_DIM // 2, 1) * sin


def _rope_partial(r, ci, sa, sb):
    q = IDX_ROPE_DIM // 2
    return r * ci + pltpu.roll(r, LANES - q, 1) * sa + pltpu.roll(r, q, 1) * sb


def _proj_a_kernel(x_ref, mod_ref, w_ref, gain_ref, cos_ref, sin_ref, o_ref, h_ref, *, n_qk, tn):
    j = pl.program_id(1)

    @pl.when(j == 0)
    def _():
        h = _modulate(x_ref[...], mod_ref[0, 3:4, :], mod_ref[0, 4:5, :])
        h_ref[...] = h.astype(BF16)

    p = _dot(h_ref[...], w_ref[...])

    @pl.when(j < n_qk)
    def _():
        cos = cos_ref[...]
        sin = sin_ref[...]
        for c in range(tn // LANES):
            sl = slice(c * LANES, (c + 1) * LANES)
            r = _rms(p[:, sl]) * gain_ref[0, :, sl]
            o_ref[:, sl] = _rope_full(r, cos, sin).astype(BF16)

    @pl.when(j >= n_qk)
    def _():
        o_ref[...] = p.astype(BF16)


def _proj_a(x2, mod, w, gains, cos, sin, seq):
    t, d = x2.shape
    n = w.shape[1]
    tm = min(512, seq)
    tn = 512
    nsb = seq // tm
    n_qk = 2 * d // tn
    return pl.pallas_call(
        functools.partial(_proj_a_kernel, n_qk=n_qk, tn=tn),
        grid=(t // tm, n // tn),
        in_specs=[
            pl.BlockSpec((tm, d), lambda i, j: (i, 0)),
            pl.BlockSpec((1, N_ADA, d), lambda i, j: (i // nsb, 0, 0)),
            pl.BlockSpec((d, tn), lambda i, j: (0, j)),
            pl.BlockSpec((1, 1, tn), lambda i, j: (j, 0, 0)),
            pl.BlockSpec((tm, LANES), lambda i, j: (i % nsb, 0)),
            pl.BlockSpec((tm, LANES), lambda i, j: (i % nsb, 0)),
        ],
        out_specs=pl.BlockSpec((tm, tn), lambda i, j: (i, j)),
        out_shape=jax.ShapeDtypeStruct((t, n), BF16),
        scratch_shapes=[pltpu.VMEM((tm, d), BF16)],
        compiler_params=_cparams(("arbitrary", "arbitrary")),
        name="proj_a",
    )(x2, mod, w, gains, cos, sin)


def _proj_b_kernel(x_ref, mod_ref, w_ref, gain_ref, cos_ref, sin_ref, ci_ref, sa_ref, sb_ref,
                   o_ref, wi_ref, h_ref, *, n_q, n_qi, tn, wi_scale):
    j = pl.program_id(1)

    @pl.when(j == 0)
    def _():
        h = _modulate(x_ref[...], mod_ref[0, 3:4, :], mod_ref[0, 4:5, :])
        h_ref[...] = h.astype(BF16)

    p = _dot(h_ref[...], w_ref[...])

    @pl.when(j < n_qi)
    def _():
        ci = ci_ref[...]
        sa = sa_ref[...]
        sb = sb_ref[...]
        for c in range(tn // LANES):
            sl = slice(c * LANES, (c + 1) * LANES)
            o_ref[:, sl] = _rope_partial(p[:, sl], ci, sa, sb).astype(BF16)

    @pl.when(jnp.logical_and(j >= n_qi, j < n_q + n_qi))
    def _():
        cos = cos_ref[...]
        sin = sin_ref[...]
        for c in range(tn // LANES):
            sl = slice(c * LANES, (c + 1) * LANES)
            r = _rms(p[:, sl]) * gain_ref[0, 0:1, sl]
            o_ref[:, sl] = _rope_full(r, cos, sin).astype(BF16)

    @pl.when(j == n_q + n_qi)
    def _():
        k = _rms(p[:, 0:LANES]) * gain_ref[0, 0:1, 0:LANES]
        o_ref[:, 0:LANES] = _rope_full(k, cos_ref[...], sin_ref[...]).astype(BF16)
        ki = p[:, LANES:2 * LANES]
        mu = jnp.mean(ki, axis=-1, keepdims=True)
        kc = ki - mu
        var = jnp.mean(kc * kc, axis=-1, keepdims=True)
        kn = kc * lax.rsqrt(var + NORM_EPS) * gain_ref[0, 0:1, LANES:2 * LANES] \
            + gain_ref[0, 1:2, LANES:2 * LANES]
        o_ref[:, LANES:2 * LANES] = _rope_partial(kn, ci_ref[...], sa_ref[...], sb_ref[...]).astype(BF16)
        o_ref[:, 2 * LANES:3 * LANES] = p[:, 2 * LANES:3 * LANES].astype(BF16)
        wi = p[:, 3 * LANES:4 * LANES] * wi_scale
        o_ref[:, 3 * LANES:4 * LANES] = wi.astype(BF16)
        wi_ref[...] = wi


def _proj_b(x2, mod, w, gains, cos, sin, ci, sa, sb, seq):
    t, d = x2.shape
    n = w.shape[1]
    tm = min(512, seq)
    tn = 512
    nsb = seq // tm
    n_q = d // tn
    n_qi = IDX_HEADS * HEAD_DIM // tn
    assert n == (n_q + n_qi + 1) * tn
    wi_scale = (IDX_HEADS ** -0.5) * (HEAD_DIM ** -0.5)
    tab = pl.BlockSpec((tm, LANES), lambda i, j: (i % nsb, 0))
    return pl.pallas_call(
        functools.partial(_proj_b_kernel, n_q=n_q, n_qi=n_qi, tn=tn, wi_scale=wi_scale),
        grid=(t // tm, n // tn),
        in_specs=[
            pl.BlockSpec((tm, d), lambda i, j: (i, 0)),
            pl.BlockSpec((1, N_ADA, d), lambda i, j: (i // nsb, 0, 0)),
            pl.BlockSpec((d, tn), lambda i, j: (0, j)),
            pl.BlockSpec((1, 2, tn), lambda i, j: (j, 0, 0)),
            tab, tab, tab, tab, tab,
        ],
        out_specs=[
            pl.BlockSpec((tm, tn), lambda i, j: (i, j)),
            pl.BlockSpec((tm, LANES), lambda i, j: (i, 0)),
        ],
        out_shape=[
            jax.ShapeDtypeStruct((t, n), BF16),
            jax.ShapeDtypeStruct((t, LANES), F32),
        ],
        scratch_shapes=[pltpu.VMEM((tm, d), BF16)],
        compiler_params=_cparams(("arbitrary", "arbitrary")),
        name="proj_b",
    )(x2, mod, w, gains, cos, sin, ci, sa, sb)


def _out_kernel(o_ref, w_ref, x_ref, mod_ref, y_ref):
    y = _dot(o_ref[...], w_ref[...])
    y_ref[...] = x_ref[...] + mod_ref[0, 5:6, :] * y


def _out_proj(o2, w, x2, mod, seq):
    t, d = x2.shape
    k = o2.shape[1]
    tm = min(512, seq)
    nsb = seq // tm
    return pl.pallas_call(
        _out_kernel,
        grid=(t // tm,),
        in_specs=[
            pl.BlockSpec((tm, k), lambda i: (i, 0)),
            pl.BlockSpec((k, d), lambda i: (0, 0)),
            pl.BlockSpec((tm, d), lambda i: (i, 0)),
            pl.BlockSpec((1, N_ADA, d), lambda i: (i // nsb, 0, 0)),
        ],
        out_specs=pl.BlockSpec((tm, d), lambda i: (i, 0)),
        out_shape=jax.ShapeDtypeStruct((t, d), F32),
        compiler_params=_cparams(("arbitrary",)),
        name="out_proj",
    )(o2, w, x2, mod)


def _softmax_step(s, m_ref, l_ref, acc_ref, v, idx):
    tk = s.shape[1]
    dv = v.shape[1]
    m_prev = m_ref[idx]
    m_new = jnp.maximum(m_prev, jnp.max(s, axis=1, keepdims=True))
    alpha = jnp.exp2(m_prev - m_new)
    p = jnp.exp2(s - _tile_lanes(m_new, tk // LANES))
    l_ref[idx] = alpha * l_ref[idx] + jnp.sum(p, axis=1, keepdims=True)
    acc_ref[idx] = _tile_lanes(alpha, dv // LANES) * acc_ref[idx] + _dot(p.astype(BF16), v)
    m_ref[idx] = m_new


def _attn_a_kernel(q0_ref, q1_ref, k0_ref, k1_ref, v_ref, lam_ref, gain_ref, o_ref,
                   m_ref, l_ref, acc_ref, *, tq, tk, lam_init):
    qi = pl.program_id(2)
    m_ref[...] = jnp.full_like(m_ref, NEG)
    l_ref[...] = jnp.zeros_like(l_ref)
    acc_ref[...] = jnp.zeros_like(acc_ref)
    q0 = q0_ref[0]
    q1 = q1_ref[0]
    n_full = (qi * tq) // tk

    def chunk(c, masked):
        off = pl.multiple_of(c * tk, tk)
        v = v_ref[0, pl.ds(off, tk), :]
        s0 = _dot_nt(q0, k0_ref[0, pl.ds(off, tk), :])
        s1 = _dot_nt(q1, k1_ref[0, pl.ds(off, tk), :])
        if masked:
            row = qi * tq + lax.broadcasted_iota(jnp.int32, (tq, tk), 0)
            col = off + lax.broadcasted_iota(jnp.int32, (tq, tk), 1)
            keep = col <= row
            s0 = jnp.where(keep, s0, NEG)
            s1 = jnp.where(keep, s1, NEG)
        _softmax_step(s0, m_ref, l_ref, acc_ref, v, 0)
        _softmax_step(s1, m_ref, l_ref, acc_ref, v, 1)

    def body(c, carry):
        chunk(c, False)
        return carry

    lax.fori_loop(0, n_full, body, 0)
    chunk(n_full, True)

    dv = acc_ref.shape[2]
    lam_q = lam_ref[...]
    s1 = jnp.sum(lam_q[0:1, :] * lam_q[1:2, :], axis=1, keepdims=True)
    s2 = jnp.sum(lam_q[2:3, :] * lam_q[3:4, :], axis=1, keepdims=True)
    lam = jnp.exp(s1) - jnp.exp(s2) + lam_init
    o0 = acc_ref[0] * _tile_lanes(1.0 / l_ref[0], dv // LANES)
    o1 = acc_ref[1] * _tile_lanes(1.0 / l_ref[1], dv // LANES)
    o = o0 - lam * o1
    o_ref[0] = (_rms(o) * gain_ref[...]).astype(BF16)


def _attn_a(p3, lam_vecs, gain_eff, n_heads, lam_init):
    b, s, n = p3.shape
    d = n // 3
    tq = min(512, s)
    tk = min(1024, s)
    hd = HEAD_DIM
    kcol = d // hd
    vcol = 2 * d // (2 * hd)
    return pl.pallas_call(
        functools.partial(_attn_a_kernel, tq=tq, tk=tk, lam_init=lam_init),
        grid=(b, n_heads, s // tq),
        in_specs=[
            pl.BlockSpec((1, tq, hd), lambda bi, h, i: (bi, i, 2 * h)),
            pl.BlockSpec((1, tq, hd), lambda bi, h, i: (bi, i, 2 * h + 1)),
            pl.BlockSpec((1, s, hd), lambda bi, h, i: (bi, 0, kcol + 2 * h)),
            pl.BlockSpec((1, s, hd), lambda bi, h, i: (bi, 0, kcol + 2 * h + 1)),
            pl.BlockSpec((1, s, 2 * hd), lambda bi, h, i: (bi, 0, vcol + h)),
            pl.BlockSpec((4, hd), lambda bi, h, i: (0, 0)),
            pl.BlockSpec((1, 2 * hd), lambda bi, h, i: (0, 0)),
        ],
        out_specs=pl.BlockSpec((1, tq, 2 * hd), lambda bi, h, i: (bi, i, h)),
        out_shape=jax.ShapeDtypeStruct((b, s, d), BF16),
        scratch_shapes=[
            pltpu.VMEM((2, tq, LANES), F32),
            pltpu.VMEM((2, tq, LANES), F32),
            pltpu.VMEM((2, tq, 2 * hd), F32),
        ],
        compiler_params=_cparams(("arbitrary", "arbitrary", "arbitrary")),
        name="attn_a",
    )(p3, p3, p3, p3, p3, lam_vecs, gain_eff)


def _attn_b_kernel(q_ref, qi_ref, wi_ref, k_ref, ki_ref, vt_ref, o_ref,
                   keys_ref, m_ref, acc_ref, *, tq, tk, n_heads, topk, idx_bits):
    i = pl.program_id(1)
    nkc = ((i + 1) * tq + tk - 1) // tk
    kf = float(topk)

    def key_pos(off, rows):
        return off + lax.broadcasted_iota(jnp.int32, (rows, tq), 0)

    def score_chunk(c, carry):
        off = pl.multiple_of(c * tk, tk)
        kic = ki_ref[0, pl.ds(off, tk), :]
        sc = jnp.zeros((tk, tq), F32)
        for h in range(IDX_HEADS):
            s = _dot_nt(kic, qi_ref[0, :, h * HEAD_DIM:(h + 1) * HEAD_DIM])
            sc = sc + wi_ref[0, h:h + 1, :] * jnp.maximum(s, 0.0)
        bits = pltpu.bitcast(sc, jnp.int32)
        key = bits ^ ((bits >> 31) & jnp.int32(0x7FFFFFFF))
        qpos = i * tq + lax.broadcasted_iota(jnp.int32, (tk, tq), 1)
        keys_ref[pl.ds(off, tk), :] = jnp.where(key_pos(off, tk) <= qpos, key, INT_MIN)
        return carry

    lax.fori_loop(0, nkc, score_chunk, 0)

    def col_total(part):
        return jnp.sum(part.astype(F32), axis=0, keepdims=True)

    def fold_chunk(off, part, hit_fn):
        for r in range(tk // ACC_CHAINS):
            ro = off + r * ACC_CHAINS
            part = part + jnp.where(hit_fn(keys_ref[pl.ds(ro, ACC_CHAINS), :], ro), 1, 0)
        return part

    def count_ge(thr):
        def cbody(c, part):
            off = pl.multiple_of(c * tk, tk)
            return fold_chunk(off, part, lambda kk, ro: kk >= thr)
        return col_total(lax.fori_loop(0, nkc, cbody, jnp.zeros((ACC_CHAINS, tq), jnp.int32)))

    def search(it, thr):
        cand = thr + jnp.left_shift(jnp.int32(1), 31 - it)
        return jnp.where(count_ge(cand) >= kf, cand, thr)

    thr = lax.fori_loop(0, 32, search, jnp.full((1, tq), INT_MIN, jnp.int32))

    tie = jnp.where(jnp.logical_and(count_ge(thr) > kf, thr > INT_MIN), 1.0, 0.0)
    any_tie = jnp.max(tie)

    @pl.when(any_tie > 0.0)
    def _():
        need = kf - count_ge(thr + 1)

        def count_eq_before(jcut):
            def cbody(c, part):
                off = pl.multiple_of(c * tk, tk)
                return fold_chunk(off, part, lambda kk, ro: jnp.logical_and(
                    kk == thr, key_pos(ro, ACC_CHAINS) < jcut))
            return col_total(lax.fori_loop(0, nkc, cbody, jnp.zeros((ACC_CHAINS, tq), jnp.int32)))

        def jsearch(it, jcut):
            cand = jcut + jnp.left_shift(jnp.int32(1), idx_bits - 1 - it)
            return jnp.where(count_eq_before(cand) <= need, cand, jcut)

        jcut = lax.fori_loop(0, idx_bits, jsearch, jnp.zeros((1, tq), jnp.int32))

        def demote(c, carry):
            off = pl.multiple_of(c * tk, tk)
            kk = keys_ref[pl.ds(off, tk), :]
            drop = jnp.logical_and(jnp.logical_and(kk == thr, key_pos(off, tk) >= jcut), tie > 0.0)
            keys_ref[pl.ds(off, tk), :] = jnp.where(drop, kk - 1, kk)
            return carry

        lax.fori_loop(0, nkc, demote, 0)

    thr_sel = jnp.maximum(thr, INT_MIN + 1)
    m_ref[...] = jnp.full_like(m_ref, NEG)
    acc_ref[...] = jnp.zeros_like(acc_ref)

    def attend(c, carry):
        off = pl.multiple_of(c * tk, tk)
        kc = k_ref[0, pl.ds(off, tk), :]
        vt = vt_ref[0, :, pl.ds(off, tk)]
        bias = jnp.where(keys_ref[pl.ds(off, tk), :] >= thr_sel, 0.0, NEG)
        for h in range(n_heads):
            s = _dot_nt(kc, q_ref[0, :, h * HEAD_DIM:(h + 1) * HEAD_DIM]) + bias
            m_prev = m_ref[h]
            m_new = jnp.maximum(m_prev, jnp.max(s, axis=0, keepdims=True))
            alpha = jnp.exp2(m_prev - m_new)
            p = jnp.exp2(s - m_new)
            acc_ref[h] = alpha * acc_ref[h] + _dot(vt, p.astype(BF16))
            m_ref[h] = m_new
        return carry

    lax.fori_loop(0, nkc, attend, 0)

    for h in range(n_heads):
        a = acc_ref[h]
        o = a[0:HEAD_DIM, :] * (1.0 / a[HEAD_DIM:HEAD_DIM + 1, :])
        o_ref[0, :, h * HEAD_DIM:(h + 1) * HEAD_DIM] = o.T.astype(BF16)


def _attn_b(p3, wi_t, v1t, n_heads, topk):
    b, s, n = p3.shape
    tq = min(256, s)
    tk = min(512, s)
    d = n_heads * HEAD_DIM
    qi_w = IDX_HEADS * HEAD_DIM
    assert qi_w % d == 0
    kcol = (d + qi_w) // HEAD_DIM
    vrows = v1t.shape[1]
    assert vrows == 2 * HEAD_DIM
    return pl.pallas_call(
        functools.partial(_attn_b_kernel, tq=tq, tk=tk, n_heads=n_heads, topk=topk,
                          idx_bits=int(s).bit_length()),
        grid=(b, s // tq),
        in_specs=[
            pl.BlockSpec((1, tq, d), lambda bi, i: (bi, i, qi_w // d)),
            pl.BlockSpec((1, tq, qi_w), lambda bi, i: (bi, i, 0)),
            pl.BlockSpec((1, LANES, tq), lambda bi, i: (bi, 0, i)),
            pl.BlockSpec((1, s, HEAD_DIM), lambda bi, i: (bi, 0, kcol)),
            pl.BlockSpec((1, s, HEAD_DIM), lambda bi, i: (bi, 0, kcol + 1)),
            pl.BlockSpec((1, vrows, s), lambda bi, i: (bi, 0, 0)),
        ],
        out_specs=pl.BlockSpec((1, tq, d), lambda bi, i: (bi, i, 0)),
        out_shape=jax.ShapeDtypeStruct((b, s, d), BF16),
        scratch_shapes=[
            pltpu.VMEM((s, tq), jnp.int32),
            pltpu.VMEM((n_heads, 1, tq), F32),
            pltpu.VMEM((n_heads, vrows, tq), F32),
        ],
        compiler_params=_cparams(("arbitrary", "arbitrary")),
        name="attn_b",
    )(p3, p3, wi_t, p3, p3, v1t)


def _rope_tables(seq):
    pos = jnp.arange(seq, dtype=F32)[:, None]

    def cs(dim):
        inv = 1.0 / (ROPE_THETA ** (jnp.arange(0, dim, 2, dtype=F32) / dim))
        ang = pos * inv[None, :]
        return jnp.cos(ang), jnp.sin(ang)

    c, s = cs(HEAD_DIM)
    cos = jnp.concatenate([c, c], axis=1)
    sin = jnp.concatenate([-s, s], axis=1)
    c, s = cs(IDX_ROPE_DIM)
    one = jnp.ones((seq, LANES - IDX_ROPE_DIM), F32)
    z32 = jnp.zeros_like(s)
    z64 = jnp.zeros_like(one)
    ci = jnp.concatenate([c, c, one], axis=1)
    sa = jnp.concatenate([-s, z32, z64], axis=1)
    sb = jnp.concatenate([z32, s, z64], axis=1)
    return cos, sin, ci, sa, sb


def kernel(x, c, ada_w, ada_b, ffn_w_gate, ffn_w_up, ffn_w_down, a_w_in, a_w_out, a_q_gain, a_k_gain,
           a_lambda_q1, a_lambda_k1, a_lambda_q2, a_lambda_k2, a_subln_gain, b_w_in, b_w_out, b_q_gain,
           b_k_gain, b_kidx_gain, b_kidx_bias):
    b, s, d = x.shape
    depth = ada_w.shape[0]
    hd = HEAD_DIM
    a_heads = d // (2 * hd)
    b_heads = d // hd
    qscale = (hd ** -0.5) * math.log2(math.e)
    topk = min(TOPK_MAX, s // 4)
    tn = 512

    cos, sin, ci, sa, sb = _rope_tables(s)
    mod = _ada_mod(c, ada_w, ada_b)
    wg = ffn_w_gate.astype(BF16)
    wu = ffn_w_up.astype(BF16)
    wd = ffn_w_down.astype(BF16)

    x2 = x.reshape(b * s, d)
    for i in range(depth):
        j = i // 2
        x2 = _ffn(x2, mod[i], wg[i, 0], wu[i, 0], wd[i, 0], 0, s)
        if i % 2 == 0:
            lam_init = 0.8 - 0.6 * math.exp(-0.3 * i)
            reps = tn // hd
            gains = jnp.concatenate([
                jnp.tile(jnp.tile(a_q_gain[j] * qscale, reps)[None], (d // tn, 1)),
                jnp.tile(jnp.tile(a_k_gain[j], reps)[None], (d // tn, 1)),
                jnp.ones((d // tn, tn), F32),
            ], axis=0)[:, None, :]
            p = _proj_a(x2, mod[i], a_w_in[j].astype(BF16), gains, cos, sin, s)
            lam_vecs = jnp.stack([a_lambda_q1[j], a_lambda_k1[j], a_lambda_q2[j], a_lambda_k2[j]])
            gain_eff = (a_subln_gain[j] * (1.0 - lam_init))[None, :]
            o = _attn_a(p.reshape(b, s, 3 * d), lam_vecs, gain_eff, a_heads, lam_init)
            x2 = _out_proj(o.reshape(b * s, d), a_w_out[j].astype(BF16), x2, mod[i], s)
        else:
            w = b_w_in[j]
            q_w = b_heads * hd
            qi_w = IDX_HEADS * hd
            o_k, o_v, o_qi, o_ki, o_wi = q_w, q_w + hd, q_w + 2 * hd, q_w + 2 * hd + qi_w, q_w + 3 * hd + qi_w
            w_re = jnp.concatenate([
                w[:, o_qi:o_qi + qi_w], w[:, :q_w], w[:, o_k:o_k + hd], w[:, o_ki:o_ki + hd],
                w[:, o_v:o_v + hd], w[:, o_wi:o_wi + IDX_HEADS],
                jnp.zeros((d, hd - IDX_HEADS), F32),
            ], axis=1).astype(BF16)
            n_tiles = w_re.shape[1] // tn
            reps = tn // hd
            g0 = jnp.zeros((n_tiles, 2, tn), F32)
            g0 = g0.at[qi_w // tn:(qi_w + q_w) // tn, 0, :].set(jnp.tile(b_q_gain[j] * qscale, reps)[None])
            g0 = g0.at[n_tiles - 1, 0, 0:hd].set(b_k_gain[j])
            g0 = g0.at[n_tiles - 1, 0, hd:2 * hd].set(b_kidx_gain[j])
            g0 = g0.at[n_tiles - 1, 1, hd:2 * hd].set(b_kidx_bias[j])
            p, wi = _proj_b(x2, mod[i], w_re, g0, cos, sin, ci, sa, sb, s)
            p3 = p.reshape(b, s, w_re.shape[1])
            v_t = jnp.swapaxes(p3[:, :, qi_w + q_w + 2 * hd:qi_w + q_w + 3 * hd], 1, 2)
            v1t = jnp.concatenate([v_t, jnp.ones_like(v_t)], axis=1)
            wi_t = jnp.swapaxes(wi.reshape(b, s, LANES), 1, 2)
            o = _attn_b(p3, wi_t, v1t, b_heads, topk)
            x2 = _out_proj(o.reshape(b * s, d), b_w_out[j].astype(BF16), x2, mod[i], s)
        x2 = _ffn(x2, mod[i], wg[i, 1], wu[i, 1], wd[i, 1], 6, s)
    return x2.reshape(b, s, d)
```

```python
import functools
import math

import jax
import jax.numpy as jnp
import numpy as np
from jax import lax
from jax.experimental import pallas as pl
from jax.experimental.pallas import tpu as pltpu

ROPE_THETA = 10000.0
NORM_EPS = 1e-6
N_ADA = 9
HEAD_DIM = 128
IDX_HEADS = 16
IDX_ROPE_DIM = 64
TOPK_MAX = 256
LANES = 128
ACC_CHAINS = 64
V7X_VMEM_LIMIT = 56 * 1024 * 1024
NEG = -1e30
INT_MIN = np.int32(-(2 ** 31))

BF16 = jnp.bfloat16
F32 = jnp.float32


def _cparams(sem):
    return pltpu.CompilerParams(dimension_semantics=sem, vmem_limit_bytes=V7X_VMEM_LIMIT)


def _dot(a, b):
    return jnp.dot(a, b, preferred_element_type=F32)


def _dot_nt(a, b):
    return lax.dot_general(a, b, (((1,), (1,)), ((), ())), preferred_element_type=F32)


def _tile_lanes(x, n):
    return x if n == 1 else jnp.concatenate([x] * n, axis=1)


def _modulate(x, shift, scale):
    ms = jnp.mean(x * x, axis=-1, keepdims=True)
    return x * lax.rsqrt(ms + NORM_EPS) * (1.0 + scale) + shift


def _rms(x):
    return x * lax.rsqrt(jnp.mean(x * x, axis=-1, keepdims=True) + NORM_EPS)


def _ada_kernel(c_ref, w_ref, b_ref, o_ref):
    c = c_ref[...]
    ca = (c * (1.0 / (1.0 + jnp.exp(-c)))).astype(BF16)
    o_ref[0] = _dot(ca, w_ref[0].astype(BF16)) + b_ref[0]


def _ada_mod(c, ada_w, ada_b):
    depth, d, n = ada_w.shape
    b = c.shape[0]
    rows = 8
    c8 = jnp.zeros((rows, d), F32).at[:b].set(c)
    tn = math.gcd(n, 1024)
    out = pl.pallas_call(
        _ada_kernel,
        grid=(depth, n // tn),
        in_specs=[
            pl.BlockSpec((rows, d), lambda i, j: (0, 0)),
            pl.BlockSpec((1, d, tn), lambda i, j: (i, 0, j)),
            pl.BlockSpec((1, 1, tn), lambda i, j: (i, 0, j)),
        ],
        out_specs=pl.BlockSpec((1, rows, tn), lambda i, j: (i, 0, j)),
        out_shape=jax.ShapeDtypeStruct((depth, rows, n), F32),
        compiler_params=_cparams(("arbitrary", "arbitrary")),
        name="ada_mod",
    )(c8, ada_w, ada_b.reshape(depth, 1, n))
    return out[:, :b].reshape(depth, b, N_ADA, d)


def _ffn_kernel(x_ref, mod_ref, wg_ref, wu_ref, wd_ref, o_ref, h_ref, *, row0, nf):
    f = pl.program_id(1)

    @pl.when(f == 0)
    def _():
        h = _modulate(x_ref[...], mod_ref[0, row0:row0 + 1, :], mod_ref[0, row0 + 1:row0 + 2, :])
        h_ref[...] = h.astype(BF16)
        o_ref[...] = jnp.zeros_like(o_ref)

    h = h_ref[...]
    a = _dot(h, wg_ref[...])
    u = _dot(h, wu_ref[...])
    act = (a * (1.0 / (1.0 + jnp.exp(-a))) * u).astype(BF16)
    o_ref[...] += _dot(act, wd_ref[...])

    @pl.when(f == nf - 1)
    def _():
        g = mod_ref[0, row0 + 2:row0 + 3, :]
        o_ref[...] = x_ref[...] + 0.5 * g * o_ref[...]


def _ffn(x2, mod, wg, wu, wd, row0, seq):
    t, d = x2.shape
    f_dim = wg.shape[1]
    tm = min(512, seq)
    tf = 512 if f_dim % 512 == 0 else f_dim
    nsb = seq // tm
    nf = f_dim // tf
    return pl.pallas_call(
        functools.partial(_ffn_kernel, row0=row0, nf=nf),
        grid=(t // tm, nf),
        in_specs=[
            pl.BlockSpec((tm, d), lambda i, f: (i, 0)),
            pl.BlockSpec((1, N_ADA, d), lambda i, f: (i // nsb, 0, 0)),
            pl.BlockSpec((d, tf), lambda i, f: (0, f)),
            pl.BlockSpec((d, tf), lambda i, f: (0, f)),
            pl.BlockSpec((tf, d), lambda i, f: (f, 0)),
        ],
        out_specs=pl.BlockSpec((tm, d), lambda i, f: (i, 0)),
        out_shape=jax.ShapeDtypeStruct((t, d), F32),
        scratch_shapes=[pltpu.VMEM((tm, d), BF16)],
        compiler_params=_cparams(("arbitrary", "arbitrary")),
        name="ffn",
    )(x2, mod, wg, wu, wd)


def _rope_full(r, cos, sin):
    return r * cos + pltpu.roll(r, HEAD_DIM // 2, 1) * sin


def _rope_partial(r, ci, sa, sb):
    q = IDX_ROPE_DIM // 2
    return r * ci + pltpu.roll(r, LANES - q, 1) * sa + pltpu.roll(r, q, 1) * sb


def _proj_a_kernel(x_ref, mod_ref, w_ref, gain_ref, cos_ref, sin_ref, o_ref, h_ref, *, n_qk, tn):
    j = pl.program_id(1)

    @pl.when(j == 0)
    def _():
        h = _modulate(x_ref[...], mod_ref[0, 3:4, :], mod_ref[0, 4:5, :])
        h_ref[...] = h.astype(BF16)

    p = _dot(h_ref[...], w_ref[...])

    @pl.when(j < n_qk)
    def _():
        cos = cos_ref[...]
        sin = sin_ref[...]
        for c in range(tn // LANES):
            sl = slice(c * LANES, (c + 1) * LANES)
            r = _rms(p[:, sl]) * gain_ref[0, :, sl]
            o_ref[:, sl] = _rope_full(r, cos, sin).astype(BF16)

    @pl.when(j >= n_qk)
    def _():
        o_ref[...] = p.astype(BF16)


def _proj_a(x2, mod, w, gains, cos, sin, seq):
    t, d = x2.shape
    n = w.shape[1]
    tm = min(512, seq)
    tn = 512
    nsb = seq // tm
    n_qk = 2 * d // tn
    return pl.pallas_call(
        functools.partial(_proj_a_kernel, n_qk=n_qk, tn=tn),
        grid=(t // tm, n // tn),
        in_specs=[
            pl.BlockSpec((tm, d), lambda i, j: (i, 0)),
            pl.BlockSpec((1, N_ADA, d), lambda i, j: (i // nsb, 0, 0)),
            pl.BlockSpec((d, tn), lambda i, j: (0, j)),
            pl.BlockSpec((1, 1, tn), lambda i, j: (j, 0, 0)),
            pl.BlockSpec((tm, LANES), lambda i, j: (i % nsb, 0)),
            pl.BlockSpec((tm, LANES), lambda i, j: (i % nsb, 0)),
        ],
        out_specs=pl.BlockSpec((tm, tn), lambda i, j: (i, j)),
        out_shape=jax.ShapeDtypeStruct((t, n), BF16),
        scratch_shapes=[pltpu.VMEM((tm, d), BF16)],
        compiler_params=_cparams(("arbitrary", "arbitrary")),
        name="proj_a",
    )(x2, mod, w, gains, cos, sin)


def _proj_b_kernel(x_ref, mod_ref, w_ref, gain_ref, cos_ref, sin_ref, ci_ref, sa_ref, sb_ref,
                   o_ref, wi_ref, h_ref, *, n_q, n_qi, tn, wi_scale):
    j = pl.program_id(1)

    @pl.when(j == 0)
    def _():
        h = _modulate(x_ref[...], mod_ref[0, 3:4, :], mod_ref[0, 4:5, :])
        h_ref[...] = h.astype(BF16)

    p = _dot(h_ref[...], w_ref[...])

    @pl.when(j < n_qi)
    def _():
        ci = ci_ref[...]
        sa = sa_ref[...]
        sb = sb_ref[...]
        for c in range(tn // LANES):
            sl = slice(c * LANES, (c + 1) * LANES)
            o_ref[:, sl] = _rope_partial(p[:, sl], ci, sa, sb).astype(BF16)

    @pl.when(jnp.logical_and(j >= n_qi, j < n_q + n_qi))
    def _():
        cos = cos_ref[...]
        sin = sin_ref[...]
        for c in range(tn // LANES):
            sl = slice(c * LANES, (c + 1) * LANES)
            r = _rms(p[:, sl]) * gain_ref[0, 0:1, sl]
            o_ref[:, sl] = _rope_full(r, cos, sin).astype(BF16)

    @pl.when(j == n_q + n_qi)
    def _():
        k = _rms(p[:, 0:LANES]) * gain_ref[0, 0:1, 0:LANES]
        o_ref[:, 0:LANES] = _rope_full(k, cos_ref[...], sin_ref[...]).astype(BF16)
        ki = p[:, LANES:2 * LANES]
        mu = jnp.mean(ki, axis=-1, keepdims=True)
        kc = ki - mu
        var = jnp.mean(kc * kc, axis=-1, keepdims=True)
        kn = kc * lax.rsqrt(var + NORM_EPS) * gain_ref[0, 0:1, LANES:2 * LANES] \
            + gain_ref[0, 1:2, LANES:2 * LANES]
        o_ref[:, LANES:2 * LANES] = _rope_partial(kn, ci_ref[...], sa_ref[...], sb_ref[...]).astype(BF16)
        o_ref[:, 2 * LANES:3 * LANES] = p[:, 2 * LANES:3 * LANES].astype(BF16)
        wi = p[:, 3 * LANES:4 * LANES] * wi_scale
        o_ref[:, 3 * LANES:4 * LANES] = wi.astype(BF16)
        wi_ref[...] = wi


def _proj_b(x2, mod, w, gains, cos, sin, ci, sa, sb, seq):
    t, d = x2.shape
    n = w.shape[1]
    tm = min(512, seq)
    tn = 512
    nsb = seq // tm
    n_q = d // tn
    n_qi = IDX_HEADS * HEAD_DIM // tn
    assert n == (n_q + n_qi + 1) * tn
    wi_scale = (IDX_HEADS ** -0.5) * (HEAD_DIM ** -0.5)
    tab = pl.BlockSpec((tm, LANES), lambda i, j: (i % nsb, 0))
    return pl.pallas_call(
        functools.partial(_proj_b_kernel, n_q=n_q, n_qi=n_qi, tn=tn, wi_scale=wi_scale),
        grid=(t // tm, n // tn),
        in_specs=[
            pl.BlockSpec((tm, d), lambda i, j: (i, 0)),
            pl.BlockSpec((1, N_ADA, d), lambda i, j: (i // nsb, 0, 0)),
            pl.BlockSpec((d, tn), lambda i, j: (0, j)),
            pl.BlockSpec((1, 2, tn), lambda i, j: (j, 0, 0)),
            tab, tab, tab, tab, tab,
        ],
        out_specs=[
            pl.BlockSpec((tm, tn), lambda i, j: (i, j)),
            pl.BlockSpec((tm, LANES), lambda i, j: (i, 0)),
        ],
        out_shape=[
            jax.ShapeDtypeStruct((t, n), BF16),
            jax.ShapeDtypeStruct((t, LANES), F32),
        ],
        scratch_shapes=[pltpu.VMEM((tm, d), BF16)],
        compiler_params=_cparams(("arbitrary", "arbitrary")),
        name="proj_b",
    )(x2, mod, w, gains, cos, sin, ci, sa, sb)


def _out_kernel(o_ref, w_ref, x_ref, mod_ref, y_ref):
    y = _dot(o_ref[...], w_ref[...])
    y_ref[...] = x_ref[...] + mod_ref[0, 5:6, :] * y


def _out_proj(o2, w, x2, mod, seq):
    t, d = x2.shape
    k = o2.shape[1]
    tm = min(512, seq)
    nsb = seq // tm
    return pl.pallas_call(
        _out_kernel,
        grid=(t // tm,),
        in_specs=[
            pl.BlockSpec((tm, k), lambda i: (i, 0)),
            pl.BlockSpec((k, d), lambda i: (0, 0)),
            pl.BlockSpec((tm, d), lambda i: (i, 0)),
            pl.BlockSpec((1, N_ADA, d), lambda i: (i // nsb, 0, 0)),
        ],
        out_specs=pl.BlockSpec((tm, d), lambda i: (i, 0)),
        out_shape=jax.ShapeDtypeStruct((t, d), F32),
        compiler_params=_cparams(("arbitrary",)),
        name="out_proj",
    )(o2, w, x2, mod)


def _softmax_step(s, m_ref, l_ref, acc_ref, v, idx):
    tk = s.shape[1]
    dv = v.shape[1]
    m_prev = m_ref[idx]
    m_new = jnp.maximum(m_prev, jnp.max(s, axis=1, keepdims=True))
    alpha = jnp.exp2(m_prev - m_new)
    p = jnp.exp2(s - _tile_lanes(m_new, tk // LANES))
    l_ref[idx] = alpha * l_ref[idx] + jnp.sum(p, axis=1, keepdims=True)
    acc_ref[idx] = _tile_lanes(alpha, dv // LANES) * acc_ref[idx] + _dot(p.astype(BF16), v)
    m_ref[idx] = m_new


def _attn_a_kernel(q0_ref, q1_ref, k0_ref, k1_ref, v_ref, lam_ref, gain_ref, o_ref,
                   m_ref, l_ref, acc_ref, *, tq, tk, lam_init):
    qi = pl.program_id(2)
    m_ref[...] = jnp.full_like(m_ref, NEG)
    l_ref[...] = jnp.zeros_like(l_ref)
    acc_ref[...] = jnp.zeros_like(acc_ref)
    q0 = q0_ref[0]
    q1 = q1_ref[0]
    n_full = (qi * tq) // tk

    def chunk(c, masked):
        off = pl.multiple_of(c * tk, tk)
        v = v_ref[0, pl.ds(off, tk), :]
        s0 = _dot_nt(q0, k0_ref[0, pl.ds(off, tk), :])
        s1 = _dot_nt(q1, k1_ref[0, pl.ds(off, tk), :])
        if masked:
            row = qi * tq + lax.broadcasted_iota(jnp.int32, (tq, tk), 0)
            col = off + lax.broadcasted_iota(jnp.int32, (tq, tk), 1)
            keep = col <= row
            s0 = jnp.where(keep, s0, NEG)
            s1 = jnp.where(keep, s1, NEG)
        _softmax_step(s0, m_ref, l_ref, acc_ref, v, 0)
        _softmax_step(s1, m_ref, l_ref, acc_ref, v, 1)

    def body(c, carry):
        chunk(c, False)
        return carry

    lax.fori_loop(0, n_full, body, 0)
    chunk(n_full, True)

    dv = acc_ref.shape[2]
    lam_q = lam_ref[...]
    s1 = jnp.sum(lam_q[0:1, :] * lam_q[1:2, :], axis=1, keepdims=True)
    s2 = jnp.sum(lam_q[2:3, :] * lam_q[3:4, :], axis=1, keepdims=True)
    lam = jnp.exp(s1) - jnp.exp(s2) + lam_init
    o0 = acc_ref[0] * _tile_lanes(1.0 / l_ref[0], dv // LANES)
    o1 = acc_ref[1] * _tile_lanes(1.0 / l_ref[1], dv // LANES)
    o = o0 - lam * o1
    o_ref[0] = (_rms(o) * gain_ref[...]).astype(BF16)


def _attn_a(p3, lam_vecs, gain_eff, n_heads, lam_init):
    b, s, n = p3.shape
    d = n // 3
    tq = min(1024, s)
    tk = min(1024, s)
    hd = HEAD_DIM
    kcol = d // hd
    vcol = 2 * d // (2 * hd)
    return pl.pallas_call(
        functools.partial(_attn_a_kernel, tq=tq, tk=tk, lam_init=lam_init),
        grid=(b, n_heads, s // tq),
        in_specs=[
            pl.BlockSpec((1, tq, hd), lambda bi, h, i: (bi, i, 2 * h)),
            pl.BlockSpec((1, tq, hd), lambda bi, h, i: (bi, i, 2 * h + 1)),
            pl.BlockSpec((1, s, hd), lambda bi, h, i: (bi, 0, kcol + 2 * h)),
            pl.BlockSpec((1, s, hd), lambda bi, h, i: (bi, 0, kcol + 2 * h + 1)),
            pl.BlockSpec((1, s, 2 * hd), lambda bi, h, i: (bi, 0, vcol + h)),
            pl.BlockSpec((4, hd), lambda bi, h, i: (0, 0)),
            pl.BlockSpec((1, 2 * hd), lambda bi, h, i: (0, 0)),
        ],
        out_specs=pl.BlockSpec((1, tq, 2 * hd), lambda bi, h, i: (bi, i, h)),
        out_shape=jax.ShapeDtypeStruct((b, s, d), BF16),
        scratch_shapes=[
            pltpu.VMEM((2, tq, LANES), F32),
            pltpu.VMEM((2, tq, LANES), F32),
            pltpu.VMEM((2, tq, 2 * hd), F32),
        ],
        compiler_params=_cparams(("arbitrary", "arbitrary", "arbitrary")),
        name="attn_a",
    )(p3, p3, p3, p3, p3, lam_vecs, gain_eff)


def _attn_b_kernel(q_ref, qi_ref, wi_ref, k_ref, ki_ref, vt_ref, o_ref,
                   keys_ref, m_ref, acc_ref, *, tq, tk, n_heads, topk, idx_bits):
    i = pl.program_id(1)
    nkc = ((i + 1) * tq + tk - 1) // tk
    kf = float(topk)

    def key_pos(off, rows):
        return off + lax.broadcasted_iota(jnp.int32, (rows, tq), 0)

    def score_chunk(c, carry):
        off = pl.multiple_of(c * tk, tk)
        kic = ki_ref[0, pl.ds(off, tk), :]
        sc = jnp.zeros((tk, tq), F32)
        for h in range(IDX_HEADS):
            s = _dot_nt(kic, qi_ref[0, :, h * HEAD_DIM:(h + 1) * HEAD_DIM])
            sc = sc + wi_ref[0, h:h + 1, :] * jnp.maximum(s, 0.0)
        bits = pltpu.bitcast(sc, jnp.int32)
        key = bits ^ ((bits >> 31) & jnp.int32(0x7FFFFFFF))
        qpos = i * tq + lax.broadcasted_iota(jnp.int32, (tk, tq), 1)
        keys_ref[pl.ds(off, tk), :] = jnp.where(key_pos(off, tk) <= qpos, key, INT_MIN)
        return carry

    lax.fori_loop(0, nkc, score_chunk, 0)

    def col_total(part):
        return jnp.sum(part.astype(F32), axis=0, keepdims=True)

    def fold_chunk(off, part, hit_fn):
        for r in range(tk // ACC_CHAINS):
            ro = off + r * ACC_CHAINS
            part = part + jnp.where(hit_fn(keys_ref[pl.ds(ro, ACC_CHAINS), :], ro), 1, 0)
        return part

    def count_ge(thr):
        def cbody(c, part):
            off = pl.multiple_of(c * tk, tk)
            return fold_chunk(off, part, lambda kk, ro: kk >= thr)
        return col_total(lax.fori_loop(0, nkc, cbody, jnp.zeros((ACC_CHAINS, tq), jnp.int32)))

    def search(it, thr):
        cand = thr + jnp.left_shift(jnp.int32(1), 31 - it)
        return jnp.where(count_ge(cand) >= kf, cand, thr)

    thr = lax.fori_loop(0, 32, search, jnp.full((1, tq), INT_MIN, jnp.int32))

    tie = jnp.where(jnp.logical_and(count_ge(thr) > kf, thr > INT_MIN), 1.0, 0.0)
    any_tie = jnp.max(tie)

    @pl.when(any_tie > 0.0)
    def _():
        need = kf - count_ge(thr + 1)

        def count_eq_before(jcut):
            def cbody(c, part):
                off = pl.multiple_of(c * tk, tk)
                return fold_chunk(off, part, lambda kk, ro: jnp.logical_and(
                    kk == thr, key_pos(ro, ACC_CHAINS) < jcut))
            return col_total(lax.fori_loop(0, nkc, cbody, jnp.zeros((ACC_CHAINS, tq), jnp.int32)))

        def jsearch(it, jcut):
            cand = jcut + jnp.left_shift(jnp.int32(1), idx_bits - 1 - it)
            return jnp.where(count_eq_before(cand) <= need, cand, jcut)

        jcut = lax.fori_loop(0, idx_bits, jsearch, jnp.zeros((1, tq), jnp.int32))

        def demote(c, carry):
            off = pl.multiple_of(c * tk, tk)
            kk = keys_ref[pl.ds(off, tk), :]
            drop = jnp.logical_and(jnp.logical_and(kk == thr, key_pos(off, tk) >= jcut), tie > 0.0)
            keys_ref[pl.ds(off, tk), :] = jnp.where(drop, kk - 1, kk)
            return carry

        lax.fori_loop(0, nkc, demote, 0)

    thr_sel = jnp.maximum(thr, INT_MIN + 1)
    m_ref[...] = jnp.full_like(m_ref, NEG)
    acc_ref[...] = jnp.zeros_like(acc_ref)

    def attend(c):
        off = pl.multiple_of(c * tk, tk)
        kc = k_ref[0, pl.ds(off, tk), :]
        vt = vt_ref[0, :, pl.ds(off, tk)]
        bias = jnp.where(keys_ref[pl.ds(off, tk), :] >= thr_sel, 0.0, NEG)
        for h in range(n_heads):
            s = _dot_nt(kc, q_ref[0, :, h * HEAD_DIM:(h + 1) * HEAD_DIM]) + bias
            m_prev = m_ref[h]
            m_new = jnp.maximum(m_prev, jnp.max(s, axis=0, keepdims=True))
            alpha = jnp.exp2(m_prev - m_new)
            p = jnp.exp2(s - m_new)
            acc_ref[h] = alpha * acc_ref[h] + _dot(vt, p.astype(BF16))
            m_ref[h] = m_new

    def attend_pair(j, carry):
        attend(2 * j)
        attend(2 * j + 1)
        return carry

    lax.fori_loop(0, nkc // 2, attend_pair, 0)

    @pl.when(nkc % 2 == 1)
    def _():
        attend(nkc - 1)

    for h in range(n_heads):
        a = acc_ref[h]
        o = a[0:HEAD_DIM, :] * (1.0 / a[HEAD_DIM:HEAD_DIM + 1, :])
        o_ref[0, :, h * HEAD_DIM:(h + 1) * HEAD_DIM] = o.T.astype(BF16)


def _attn_b(p3, wi_t, v1t, n_heads, topk):
    b, s, n = p3.shape
    tq = min(256, s)
    tk = min(512, s)
    d = n_heads * HEAD_DIM
    qi_w = IDX_HEADS * HEAD_DIM
    assert qi_w % d == 0
    kcol = (d + qi_w) // HEAD_DIM
    vrows = v1t.shape[1]
    assert vrows == 2 * HEAD_DIM
    return pl.pallas_call(
        functools.partial(_attn_b_kernel, tq=tq, tk=tk, n_heads=n_heads, topk=topk,
                          idx_bits=int(s).bit_length()),
        grid=(b, s // tq),
        in_specs=[
            pl.BlockSpec((1, tq, d), lambda bi, i: (bi, i, qi_w // d)),
            pl.BlockSpec((1, tq, qi_w), lambda bi, i: (bi, i, 0)),
            pl.BlockSpec((1, LANES, tq), lambda bi, i: (bi, 0, i)),
            pl.BlockSpec((1, s, HEAD_DIM), lambda bi, i: (bi, 0, kcol)),
            pl.BlockSpec((1, s, HEAD_DIM), lambda bi, i: (bi, 0, kcol + 1)),
            pl.BlockSpec((1, vrows, s), lambda bi, i: (bi, 0, 0)),
        ],
        out_specs=pl.BlockSpec((1, tq, d), lambda bi, i: (bi, i, 0)),
        out_shape=jax.ShapeDtypeStruct((b, s, d), BF16),
        scratch_shapes=[
            pltpu.VMEM((s, tq), jnp.int32),
            pltpu.VMEM((n_heads, 1, tq), F32),
            pltpu.VMEM((n_heads, vrows, tq), F32),
        ],
        compiler_params=_cparams(("arbitrary", "arbitrary")),
        name="attn_b",
    )(p3, p3, wi_t, p3, p3, v1t)


def _rope_tables(seq):
    pos = jnp.arange(seq, dtype=F32)[:, None]

    def cs(dim):
        inv = 1.0 / (ROPE_THETA ** (jnp.arange(0, dim, 2, dtype=F32) / dim))
        ang = pos * inv[None, :]
        return jnp.cos(ang), jnp.sin(ang)

    c, s = cs(HEAD_DIM)
    cos = jnp.concatenate([c, c], axis=1)
    sin = jnp.concatenate([-s, s], axis=1)
    c, s = cs(IDX_ROPE_DIM)
    one = jnp.ones((seq, LANES - IDX_ROPE_DIM), F32)
    z32 = jnp.zeros_like(s)
    z64 = jnp.zeros_like(one)
    ci = jnp.concatenate([c, c, one], axis=1)
    sa = jnp.concatenate([-s, z32, z64], axis=1)
    sb = jnp.concatenate([z32, s, z64], axis=1)
    return cos, sin, ci, sa, sb


def kernel(x, c, ada_w, ada_b, ffn_w_gate, ffn_w_up, ffn_w_down, a_w_in, a_w_out, a_q_gain, a_k_gain,
           a_lambda_q1, a_lambda_k1, a_lambda_q2, a_lambda_k2, a_subln_gain, b_w_in, b_w_out, b_q_gain,
           b_k_gain, b_kidx_gain, b_kidx_bias):
    b, s, d = x.shape
    depth = ada_w.shape[0]
    hd = HEAD_DIM
    a_heads = d // (2 * hd)
    b_heads = d // hd
    qscale = (hd ** -0.5) * math.log2(math.e)
    topk = min(TOPK_MAX, s // 4)
    tn = 512

    cos, sin, ci, sa, sb = _rope_tables(s)
    mod = _ada_mod(c, ada_w, ada_b)
    wg = ffn_w_gate.astype(BF16)
    wu = ffn_w_up.astype(BF16)
    wd = ffn_w_down.astype(BF16)

    x2 = x.reshape(b * s, d)
    for i in range(depth):
        j = i // 2
        x2 = _ffn(x2, mod[i], wg[i, 0], wu[i, 0], wd[i, 0], 0, s)
        if i % 2 == 0:
            lam_init = 0.8 - 0.6 * math.exp(-0.3 * i)
            reps = tn // hd
            gains = jnp.concatenate([
                jnp.tile(jnp.tile(a_q_gain[j] * qscale, reps)[None], (d // tn, 1)),
                jnp.tile(jnp.tile(a_k_gain[j], reps)[None], (d // tn, 1)),
                jnp.ones((d // tn, tn), F32),
            ], axis=0)[:, None, :]
            p = _proj_a(x2, mod[i], a_w_in[j].astype(BF16), gains, cos, sin, s)
            lam_vecs = jnp.stack([a_lambda_q1[j], a_lambda_k1[j], a_lambda_q2[j], a_lambda_k2[j]])
            gain_eff = (a_subln_gain[j] * (1.0 - lam_init))[None, :]
            o = _attn_a(p.reshape(b, s, 3 * d), lam_vecs, gain_eff, a_heads, lam_init)
            x2 = _out_proj(o.reshape(b * s, d), a_w_out[j].astype(BF16), x2, mod[i], s)
        else:
            w = b_w_in[j]
            q_w = b_heads * hd
            qi_w = IDX_HEADS * hd
            o_k, o_v, o_qi, o_ki, o_wi = q_w, q_w + hd, q_w + 2 * hd, q_w + 2 * hd + qi_w, q_w + 3 * hd + qi_w
            w_re = jnp.concatenate([
                w[:, o_qi:o_qi + qi_w], w[:, :q_w], w[:, o_k:o_k + hd], w[:, o_ki:o_ki + hd],
                w[:, o_v:o_v + hd], w[:, o_wi:o_wi + IDX_HEADS],
                jnp.zeros((d, hd - IDX_HEADS), F32),
            ], axis=1).astype(BF16)
            n_tiles = w_re.shape[1] // tn
            reps = tn // hd
            g0 = jnp.zeros((n_tiles, 2, tn), F32)
            g0 = g0.at[qi_w // tn:(qi_w + q_w) // tn, 0, :].set(jnp.tile(b_q_gain[j] * qscale, reps)[None])
            g0 = g0.at[n_tiles - 1, 0, 0:hd].set(b_k_gain[j])
            g0 = g0.at[n_tiles - 1, 0, hd:2 * hd].set(b_kidx_gain[j])
            g0 = g0.at[n_tiles - 1, 1, hd:2 * hd].set(b_kidx_bias[j])
            p, wi = _proj_b(x2, mod[i], w_re, g0, cos, sin, ci, sa, sb, s)
            p3 = p.reshape(b, s, w_re.shape[1])
            v_t = jnp.swapaxes(p3[:, :, qi_w + q_w + 2 * hd:qi_w + q_w + 3 * hd], 1, 2)
            v1t = jnp.concatenate([v_t, jnp.ones_like(v_t)], axis=1)
            wi_t = jnp.swapaxes(wi.reshape(b, s, LANES), 1, 2)
            o = _attn_b(p3, wi_t, v1t, b_heads, topk)
            x2 = _out_proj(o.reshape(b * s, d), b_w_out[j].astype(BF16), x2, mod[i], s)
        x2 = _ffn(x2, mod[i], wg[i, 1], wu[i, 1], wd[i, 1], 6, s)
    return x2.reshape(b, s, d)
```

```python
import functools
import math

import jax
import jax.numpy as jnp
import numpy as np
from jax import lax
from jax.experimental import pallas as pl
from jax.experimental.pallas import tpu as pltpu

ROPE_THETA = 10000.0
NORM_EPS = 1e-6
N_ADA = 9
HEAD_DIM = 128
IDX_HEADS = 16
IDX_ROPE_DIM = 64
TOPK_MAX = 256
LANES = 128
ACC_CHAINS = 64
V7X_VMEM_LIMIT = 56 * 1024 * 1024
NEG = -1e30
INT_MIN = np.int32(-(2 ** 31))

BF16 = jnp.bfloat16
F32 = jnp.float32


def _cparams(sem):
    return pltpu.CompilerParams(dimension_semantics=sem, vmem_limit_bytes=V7X_VMEM_LIMIT)


def _dot(a, b):
    return jnp.dot(a, b, preferred_element_type=F32)


def _dot_nt(a, b):
    return lax.dot_general(a, b, (((1,), (1,)), ((), ())), preferred_element_type=F32)


def _tile_lanes(x, n):
    return x if n == 1 else jnp.concatenate([x] * n, axis=1)


def _modulate(x, shift, scale):
    ms = jnp.mean(x * x, axis=-1, keepdims=True)
    return x * lax.rsqrt(ms + NORM_EPS) * (1.0 + scale) + shift


def _rms(x):
    return x * lax.rsqrt(jnp.mean(x * x, axis=-1, keepdims=True) + NORM_EPS)


def _ada_kernel(c_ref, w_ref, b_ref, o_ref):
    c = c_ref[...]
    ca = (c * (1.0 / (1.0 + jnp.exp(-c)))).astype(BF16)
    o_ref[0] = _dot(ca, w_ref[0].astype(BF16)) + b_ref[0]


def _ada_mod(c, ada_w, ada_b):
    depth, d, n = ada_w.shape
    b = c.shape[0]
    rows = 8
    c8 = jnp.zeros((rows, d), F32).at[:b].set(c)
    tn = math.gcd(n, 1024)
    out = pl.pallas_call(
        _ada_kernel,
        grid=(depth, n // tn),
        in_specs=[
            pl.BlockSpec((rows, d), lambda i, j: (0, 0)),
            pl.BlockSpec((1, d, tn), lambda i, j: (i, 0, j)),
            pl.BlockSpec((1, 1, tn), lambda i, j: (i, 0, j)),
        ],
        out_specs=pl.BlockSpec((1, rows, tn), lambda i, j: (i, 0, j)),
        out_shape=jax.ShapeDtypeStruct((depth, rows, n), F32),
        compiler_params=_cparams(("arbitrary", "arbitrary")),
        name="ada_mod",
    )(c8, ada_w, ada_b.reshape(depth, 1, n))
    return out[:, :b].reshape(depth, b, N_ADA, d)


def _ffn_kernel(x_ref, mod_ref, wg_ref, wu_ref, wd_ref, o_ref, h_ref, *, row0, nf):
    f = pl.program_id(1)

    @pl.when(f == 0)
    def _():
        h = _modulate(x_ref[...], mod_ref[0, row0:row0 + 1, :], mod_ref[0, row0 + 1:row0 + 2, :])
        h_ref[...] = h.astype(BF16)
        o_ref[...] = jnp.zeros_like(o_ref)

    h = h_ref[...]
    a = _dot(h, wg_ref[...])
    u = _dot(h, wu_ref[...])
    act = (a * (1.0 / (1.0 + jnp.exp(-a))) * u).astype(BF16)
    o_ref[...] += _dot(act, wd_ref[...])

    @pl.when(f == nf - 1)
    def _():
        g = mod_ref[0, row0 + 2:row0 + 3, :]
        o_ref[...] = x_ref[...] + 0.5 * g * o_ref[...]


def _ffn(x2, mod, wg, wu, wd, layer, which, seq):
    t, d = x2.shape
    f_dim = wg.shape[3]
    tm = min(512, seq)
    tf = 512 if f_dim % 512 == 0 else f_dim
    nsb = seq // tm
    nf = f_dim // tf
    row0 = 6 * which
    return pl.pallas_call(
        functools.partial(_ffn_kernel, row0=row0, nf=nf),
        grid=(t // tm, nf),
        in_specs=[
            pl.BlockSpec((tm, d), lambda i, f: (i, 0)),
            pl.BlockSpec((1, N_ADA, d), lambda i, f: (i // nsb, 0, 0)),
            pl.BlockSpec((None, None, d, tf), lambda i, f: (layer, which, 0, f)),
            pl.BlockSpec((None, None, d, tf), lambda i, f: (layer, which, 0, f)),
            pl.BlockSpec((None, None, tf, d), lambda i, f: (layer, which, f, 0)),
        ],
        out_specs=pl.BlockSpec((tm, d), lambda i, f: (i, 0)),
        out_shape=jax.ShapeDtypeStruct((t, d), F32),
        scratch_shapes=[pltpu.VMEM((tm, d), BF16)],
        compiler_params=_cparams(("arbitrary", "arbitrary")),
        name="ffn",
    )(x2, mod, wg, wu, wd)


def _rope_full(r, cos, sin):
    return r * cos + pltpu.roll(r, HEAD_DIM // 2, 1) * sin


def _rope_partial(r, ci, sa, sb):
    q = IDX_ROPE_DIM // 2
    return r * ci + pltpu.roll(r, LANES - q, 1) * sa + pltpu.roll(r, q, 1) * sb


def _proj_a_kernel(x_ref, mod_ref, w_ref, gain_ref, cos_ref, sin_ref, o_ref, h_ref, *, n_qk, tn):
    j = pl.program_id(1)

    @pl.when(j == 0)
    def _():
        h = _modulate(x_ref[...], mod_ref[0, 3:4, :], mod_ref[0, 4:5, :])
        h_ref[...] = h.astype(BF16)

    p = _dot(h_ref[...], w_ref[...])

    @pl.when(j < n_qk)
    def _():
        cos = cos_ref[...]
        sin = sin_ref[...]
        for c in range(tn // LANES):
            sl = slice(c * LANES, (c + 1) * LANES)
            r = _rms(p[:, sl]) * gain_ref[0, :, sl]
            o_ref[:, sl] = _rope_full(r, cos, sin).astype(BF16)

    @pl.when(j >= n_qk)
    def _():
        o_ref[...] = p.astype(BF16)


def _proj_a(x2, mod, w, gains, cos, sin, seq):
    t, d = x2.shape
    n = w.shape[1]
    tm = min(512, seq)
    tn = 512
    nsb = seq // tm
    n_qk = 2 * d // tn
    return pl.pallas_call(
        functools.partial(_proj_a_kernel, n_qk=n_qk, tn=tn),
        grid=(t // tm, n // tn),
        in_specs=[
            pl.BlockSpec((tm, d), lambda i, j: (i, 0)),
            pl.BlockSpec((1, N_ADA, d), lambda i, j: (i // nsb, 0, 0)),
            pl.BlockSpec((d, tn), lambda i, j: (0, j)),
            pl.BlockSpec((1, 1, tn), lambda i, j: (j, 0, 0)),
            pl.BlockSpec((tm, LANES), lambda i, j: (i % nsb, 0)),
            pl.BlockSpec((tm, LANES), lambda i, j: (i % nsb, 0)),
        ],
        out_specs=pl.BlockSpec((tm, tn), lambda i, j: (i, j)),
        out_shape=jax.ShapeDtypeStruct((t, n), BF16),
        scratch_shapes=[pltpu.VMEM((tm, d), BF16)],
        compiler_params=_cparams(("arbitrary", "arbitrary")),
        name="proj_a",
    )(x2, mod, w, gains, cos, sin)


def _proj_b_kernel(x_ref, mod_ref, w_ref, gain_ref, cos_ref, sin_ref, ci_ref, sa_ref, sb_ref,
                   o_ref, wi_ref, h_ref, *, n_q, n_qi, tn, wi_scale):
    j = pl.program_id(1)

    @pl.when(j == 0)
    def _():
        h = _modulate(x_ref[...], mod_ref[0, 3:4, :], mod_ref[0, 4:5, :])
        h_ref[...] = h.astype(BF16)

    p = _dot(h_ref[...], w_ref[...])

    @pl.when(j < n_qi)
    def _():
        ci = ci_ref[...]
        sa = sa_ref[...]
        sb = sb_ref[...]
        for c in range(tn // LANES):
            sl = slice(c * LANES, (c + 1) * LANES)
            o_ref[:, sl] = _rope_partial(p[:, sl], ci, sa, sb).astype(BF16)

    @pl.when(jnp.logical_and(j >= n_qi, j < n_q + n_qi))
    def _():
        cos = cos_ref[...]
        sin = sin_ref[...]
        for c in range(tn // LANES):
            sl = slice(c * LANES, (c + 1) * LANES)
            r = _rms(p[:, sl]) * gain_ref[0, 0:1, sl]
            o_ref[:, sl] = _rope_full(r, cos, sin).astype(BF16)

    @pl.when(j == n_q + n_qi)
    def _():
        k = _rms(p[:, 0:LANES]) * gain_ref[0, 0:1, 0:LANES]
        o_ref[:, 0:LANES] = _rope_full(k, cos_ref[...], sin_ref[...]).astype(BF16)
        ki = p[:, LANES:2 * LANES]
        mu = jnp.mean(ki, axis=-1, keepdims=True)
        kc = ki - mu
        var = jnp.mean(kc * kc, axis=-1, keepdims=True)
        kn = kc * lax.rsqrt(var + NORM_EPS) * gain_ref[0, 0:1, LANES:2 * LANES] \
            + gain_ref[0, 1:2, LANES:2 * LANES]
        o_ref[:, LANES:2 * LANES] = _rope_partial(kn, ci_ref[...], sa_ref[...], sb_ref[...]).astype(BF16)
        o_ref[:, 2 * LANES:3 * LANES] = p[:, 2 * LANES:3 * LANES].astype(BF16)
        wi = p[:, 3 * LANES:4 * LANES] * wi_scale
        o_ref[:, 3 * LANES:4 * LANES] = wi.astype(BF16)
        wi_ref[...] = wi


def _proj_b(x2, mod, w, gains, cos, sin, ci, sa, sb, seq):
    t, d = x2.shape
    n = w.shape[1]
    tm = min(512, seq)
    tn = 512
    nsb = seq // tm
    n_q = d // tn
    n_qi = IDX_HEADS * HEAD_DIM // tn
    assert n == (n_q + n_qi + 1) * tn
    wi_scale = (IDX_HEADS ** -0.5) * (HEAD_DIM ** -0.5)
    tab = pl.BlockSpec((tm, LANES), lambda i, j: (i % nsb, 0))
    return pl.pallas_call(
        functools.partial(_proj_b_kernel, n_q=n_q, n_qi=n_qi, tn=tn, wi_scale=wi_scale),
        grid=(t // tm, n // tn),
        in_specs=[
            pl.BlockSpec((tm, d), lambda i, j: (i, 0)),
            pl.BlockSpec((1, N_ADA, d), lambda i, j: (i // nsb, 0, 0)),
            pl.BlockSpec((d, tn), lambda i, j: (0, j)),
            pl.BlockSpec((1, 2, tn), lambda i, j: (j, 0, 0)),
            tab, tab, tab, tab, tab,
        ],
        out_specs=[
            pl.BlockSpec((tm, tn), lambda i, j: (i, j)),
            pl.BlockSpec((tm, LANES), lambda i, j: (i, 0)),
        ],
        out_shape=[
            jax.ShapeDtypeStruct((t, n), BF16),
            jax.ShapeDtypeStruct((t, LANES), F32),
        ],
        scratch_shapes=[pltpu.VMEM((tm, d), BF16)],
        compiler_params=_cparams(("arbitrary", "arbitrary")),
        name="proj_b",
    )(x2, mod, w, gains, cos, sin, ci, sa, sb)


def _out_kernel(o_ref, w_ref, x_ref, mod_ref, y_ref):
    y = _dot(o_ref[...], w_ref[...])
    y_ref[...] = x_ref[...] + mod_ref[0, 5:6, :] * y


def _out_proj(o2, w, x2, mod, seq):
    t, d = x2.shape
    k = o2.shape[1]
    tm = min(512, seq)
    nsb = seq // tm
    return pl.pallas_call(
        _out_kernel,
        grid=(t // tm,),
        in_specs=[
            pl.BlockSpec((tm, k), lambda i: (i, 0)),
            pl.BlockSpec((k, d), lambda i: (0, 0)),
            pl.BlockSpec((tm, d), lambda i: (i, 0)),
            pl.BlockSpec((1, N_ADA, d), lambda i: (i // nsb, 0, 0)),
        ],
        out_specs=pl.BlockSpec((tm, d), lambda i: (i, 0)),
        out_shape=jax.ShapeDtypeStruct((t, d), F32),
        compiler_params=_cparams(("arbitrary",)),
        name="out_proj",
    )(o2, w, x2, mod)


def _softmax_step(s, m_ref, l_ref, acc_ref, v, idx, rows=slice(None)):
    tk = s.shape[1]
    dv = v.shape[1]
    m_prev = m_ref[idx, rows, :]
    m_new = jnp.maximum(m_prev, jnp.max(s, axis=1, keepdims=True))
    alpha = jnp.exp2(m_prev - m_new)
    p = jnp.exp2(s - _tile_lanes(m_new, tk // LANES))
    l_ref[idx, rows, :] = alpha * l_ref[idx, rows, :] + jnp.sum(p, axis=1, keepdims=True)
    acc_ref[idx, rows, :] = _tile_lanes(alpha, dv // LANES) * acc_ref[idx, rows, :] + _dot(p.astype(BF16), v)
    m_ref[idx, rows, :] = m_new


def _attn_a_kernel(q0_ref, q1_ref, k0_ref, k1_ref, v_ref, lam_ref, gain_ref, o_ref,
                   m_ref, l_ref, acc_ref, *, tq, lam_init):
    qi = pl.program_id(2)
    m_ref[...] = jnp.full_like(m_ref, NEG)
    l_ref[...] = jnp.zeros_like(l_ref)
    acc_ref[...] = jnp.zeros_like(acc_ref)
    tk = tq
    q0 = q0_ref[0]
    q1 = q1_ref[0]

    def body(c, carry):
        off = pl.multiple_of(c * tk, tk)
        v = v_ref[0, pl.ds(off, tk), :]
        s0 = _dot_nt(q0, k0_ref[0, pl.ds(off, tk), :])
        s1 = _dot_nt(q1, k1_ref[0, pl.ds(off, tk), :])
        _softmax_step(s0, m_ref, l_ref, acc_ref, v, 0)
        _softmax_step(s1, m_ref, l_ref, acc_ref, v, 1)
        return carry

    lax.fori_loop(0, qi, body, 0)

    off = pl.multiple_of(qi * tk, tk)
    band = tq // 2
    for r in range(2):
        rows = slice(r * band, (r + 1) * band)
        width = (r + 1) * band
        v = v_ref[0, pl.ds(off, width), :]
        row = r * band + lax.broadcasted_iota(jnp.int32, (band, width), 0)
        keep = lax.broadcasted_iota(jnp.int32, (band, width), 1) <= row
        s0 = _dot_nt(q0_ref[0, rows, :], k0_ref[0, pl.ds(off, width), :])
        s1 = _dot_nt(q1_ref[0, rows, :], k1_ref[0, pl.ds(off, width), :])
        _softmax_step(jnp.where(keep, s0, NEG), m_ref, l_ref, acc_ref, v, 0, rows)
        _softmax_step(jnp.where(keep, s1, NEG), m_ref, l_ref, acc_ref, v, 1, rows)

    dv = acc_ref.shape[2]
    lam_q = lam_ref[...]
    s1 = jnp.sum(lam_q[0:1, :] * lam_q[1:2, :], axis=1, keepdims=True)
    s2 = jnp.sum(lam_q[2:3, :] * lam_q[3:4, :], axis=1, keepdims=True)
    lam = jnp.exp(s1) - jnp.exp(s2) + lam_init
    o0 = acc_ref[0] * _tile_lanes(1.0 / l_ref[0], dv // LANES)
    o1 = acc_ref[1] * _tile_lanes(1.0 / l_ref[1], dv // LANES)
    o = o0 - lam * o1
    o_ref[0] = (_rms(o) * gain_ref[...]).astype(BF16)


def _attn_a(p3, lam_vecs, gain_eff, n_heads, lam_init):
    b, s, n = p3.shape
    d = n // 3
    tq = min(1024, s)
    hd = HEAD_DIM
    kcol = d // hd
    vcol = 2 * d // (2 * hd)
    return pl.pallas_call(
        functools.partial(_attn_a_kernel, tq=tq, lam_init=lam_init),
        grid=(b, n_heads, s // tq),
        in_specs=[
            pl.BlockSpec((1, tq, hd), lambda bi, h, i: (bi, i, 2 * h)),
            pl.BlockSpec((1, tq, hd), lambda bi, h, i: (bi, i, 2 * h + 1)),
            pl.BlockSpec((1, s, hd), lambda bi, h, i: (bi, 0, kcol + 2 * h)),
            pl.BlockSpec((1, s, hd), lambda bi, h, i: (bi, 0, kcol + 2 * h + 1)),
            pl.BlockSpec((1, s, 2 * hd), lambda bi, h, i: (bi, 0, vcol + h)),
            pl.BlockSpec((4, hd), lambda bi, h, i: (0, 0)),
            pl.BlockSpec((1, 2 * hd), lambda bi, h, i: (0, 0)),
        ],
        out_specs=pl.BlockSpec((1, tq, 2 * hd), lambda bi, h, i: (bi, i, h)),
        out_shape=jax.ShapeDtypeStruct((b, s, d), BF16),
        scratch_shapes=[
            pltpu.VMEM((2, tq, LANES), F32),
            pltpu.VMEM((2, tq, LANES), F32),
            pltpu.VMEM((2, tq, 2 * hd), F32),
        ],
        compiler_params=_cparams(("arbitrary", "arbitrary", "arbitrary")),
        name="attn_a",
    )(p3, p3, p3, p3, p3, lam_vecs, gain_eff)


def _attn_b_kernel(q_ref, qi_ref, wi_ref, k_ref, ki_ref, vt_ref, o_ref,
                   keys_ref, m_ref, acc_ref, *, tq, tk, n_heads, topk, idx_bits):
    i = pl.program_id(1)
    nkc = ((i + 1) * tq + tk - 1) // tk
    kf = float(topk)

    def key_pos(off, rows):
        return off + lax.broadcasted_iota(jnp.int32, (rows, tq), 0)

    def score_chunk(c, carry):
        off = pl.multiple_of(c * tk, tk)
        kic = ki_ref[0, pl.ds(off, tk), :]
        sc = jnp.zeros((tk, tq), F32)
        for h in range(IDX_HEADS):
            s = _dot_nt(kic, qi_ref[0, :, h * HEAD_DIM:(h + 1) * HEAD_DIM])
            sc = sc + wi_ref[0, h:h + 1, :] * jnp.maximum(s, 0.0)
        bits = pltpu.bitcast(sc, jnp.int32)
        key = bits ^ ((bits >> 31) & jnp.int32(0x7FFFFFFF))
        qpos = i * tq + lax.broadcasted_iota(jnp.int32, (tk, tq), 1)
        keys_ref[pl.ds(off, tk), :] = jnp.where(key_pos(off, tk) <= qpos, key, INT_MIN)
        return carry

    lax.fori_loop(0, nkc, score_chunk, 0)

    def col_total(part):
        return jnp.sum(part.astype(F32), axis=0, keepdims=True)

    def fold_chunk(off, part, hit_fn):
        for r in range(tk // ACC_CHAINS):
            ro = off + r * ACC_CHAINS
            part = part + jnp.where(hit_fn(keys_ref[pl.ds(ro, ACC_CHAINS), :], ro), 1, 0)
        return part

    def count_ge(thr):
        def cbody(c, part):
            off = pl.multiple_of(c * tk, tk)
            return fold_chunk(off, part, lambda kk, ro: kk >= thr)
        return col_total(lax.fori_loop(0, nkc, cbody, jnp.zeros((ACC_CHAINS, tq), jnp.int32)))

    def search(it, thr):
        cand = thr + jnp.left_shift(jnp.int32(1), 31 - it)
        return jnp.where(count_ge(cand) >= kf, cand, thr)

    thr = lax.fori_loop(0, 32, search, jnp.full((1, tq), INT_MIN, jnp.int32))

    tie = jnp.where(jnp.logical_and(count_ge(thr) > kf, thr > INT_MIN), 1.0, 0.0)
    any_tie = jnp.max(tie)

    @pl.when(any_tie > 0.0)
    def _():
        need = kf - count_ge(thr + 1)

        def count_eq_before(jcut):
            def cbody(c, part):
                off = pl.multiple_of(c * tk, tk)
                return fold_chunk(off, part, lambda kk, ro: jnp.logical_and(
                    kk == thr, key_pos(ro, ACC_CHAINS) < jcut))
            return col_total(lax.fori_loop(0, nkc, cbody, jnp.zeros((ACC_CHAINS, tq), jnp.int32)))

        def jsearch(it, jcut):
            cand = jcut + jnp.left_shift(jnp.int32(1), idx_bits - 1 - it)
            return jnp.where(count_eq_before(cand) <= need, cand, jcut)

        jcut = lax.fori_loop(0, idx_bits, jsearch, jnp.zeros((1, tq), jnp.int32))

        def demote(c, carry):
            off = pl.multiple_of(c * tk, tk)
            kk = keys_ref[pl.ds(off, tk), :]
            drop = jnp.logical_and(jnp.logical_and(kk == thr, key_pos(off, tk) >= jcut), tie > 0.0)
            keys_ref[pl.ds(off, tk), :] = jnp.where(drop, kk - 1, kk)
            return carry

        lax.fori_loop(0, nkc, demote, 0)

    thr_sel = jnp.maximum(thr, INT_MIN + 1)
    m_ref[...] = jnp.full_like(m_ref, NEG)
    acc_ref[...] = jnp.zeros_like(acc_ref)

    def attend(c):
        off = pl.multiple_of(c * tk, tk)
        kc = k_ref[0, pl.ds(off, tk), :]
        vt = vt_ref[0, :, pl.ds(off, tk)]
        bias = jnp.where(keys_ref[pl.ds(off, tk), :] >= thr_sel, 0.0, NEG)
        for h in range(n_heads):
            s = _dot_nt(kc, q_ref[0, :, h * HEAD_DIM:(h + 1) * HEAD_DIM]) + bias
            m_prev = m_ref[h]
            m_new = jnp.maximum(m_prev, jnp.max(s, axis=0, keepdims=True))
            alpha = jnp.exp2(m_prev - m_new)
            p = jnp.exp2(s - m_new)
            acc_ref[h] = alpha * acc_ref[h] + _dot(vt, p.astype(BF16))
            m_ref[h] = m_new

    def attend_pair(j, carry):
        attend(2 * j)
        attend(2 * j + 1)
        return carry

    lax.fori_loop(0, nkc // 2, attend_pair, 0)

    @pl.when(nkc % 2 == 1)
    def _():
        attend(nkc - 1)

    for h in range(n_heads):
        a = acc_ref[h]
        o = a[0:HEAD_DIM, :] * (1.0 / a[HEAD_DIM:HEAD_DIM + 1, :])
        o_ref[0, :, h * HEAD_DIM:(h + 1) * HEAD_DIM] = o.T.astype(BF16)


def _attn_b(p3, wi_t, v1t, n_heads, topk):
    b, s, n = p3.shape
    tq = min(256, s)
    tk = min(512, s)
    d = n_heads * HEAD_DIM
    qi_w = IDX_HEADS * HEAD_DIM
    assert qi_w % d == 0
    kcol = (d + qi_w) // HEAD_DIM
    vrows = v1t.shape[1]
    assert vrows == 2 * HEAD_DIM
    return pl.pallas_call(
        functools.partial(_attn_b_kernel, tq=tq, tk=tk, n_heads=n_heads, topk=topk,
                          idx_bits=int(s).bit_length()),
        grid=(b, s // tq),
        in_specs=[
            pl.BlockSpec((1, tq, d), lambda bi, i: (bi, i, qi_w // d)),
            pl.BlockSpec((1, tq, qi_w), lambda bi, i: (bi, i, 0)),
            pl.BlockSpec((1, LANES, tq), lambda bi, i: (bi, 0, i)),
            pl.BlockSpec((1, s, HEAD_DIM), lambda bi, i: (bi, 0, kcol)),
            pl.BlockSpec((1, s, HEAD_DIM), lambda bi, i: (bi, 0, kcol + 1)),
            pl.BlockSpec((1, vrows, s), lambda bi, i: (bi, 0, 0)),
        ],
        out_specs=pl.BlockSpec((1, tq, d), lambda bi, i: (bi, i, 0)),
        out_shape=jax.ShapeDtypeStruct((b, s, d), BF16),
        scratch_shapes=[
            pltpu.VMEM((s, tq), jnp.int32),
            pltpu.VMEM((n_heads, 1, tq), F32),
            pltpu.VMEM((n_heads, vrows, tq), F32),
        ],
        compiler_params=_cparams(("arbitrary", "arbitrary")),
        name="attn_b",
    )(p3, p3, wi_t, p3, p3, v1t)


def _rope_tables(seq):
    pos = jnp.arange(seq, dtype=F32)[:, None]

    def cs(dim):
        inv = 1.0 / (ROPE_THETA ** (jnp.arange(0, dim, 2, dtype=F32) / dim))
        ang = pos * inv[None, :]
        return jnp.cos(ang), jnp.sin(ang)

    c, s = cs(HEAD_DIM)
    cos = jnp.concatenate([c, c], axis=1)
    sin = jnp.concatenate([-s, s], axis=1)
    c, s = cs(IDX_ROPE_DIM)
    one = jnp.ones((seq, LANES - IDX_ROPE_DIM), F32)
    z32 = jnp.zeros_like(s)
    z64 = jnp.zeros_like(one)
    ci = jnp.concatenate([c, c, one], axis=1)
    sa = jnp.concatenate([-s, z32, z64], axis=1)
    sb = jnp.concatenate([z32, s, z64], axis=1)
    return cos, sin, ci, sa, sb


def kernel(x, c, ada_w, ada_b, ffn_w_gate, ffn_w_up, ffn_w_down, a_w_in, a_w_out, a_q_gain, a_k_gain,
           a_lambda_q1, a_lambda_k1, a_lambda_q2, a_lambda_k2, a_subln_gain, b_w_in, b_w_out, b_q_gain,
           b_k_gain, b_kidx_gain, b_kidx_bias):
    b, s, d = x.shape
    depth = ada_w.shape[0]
    hd = HEAD_DIM
    a_heads = d // (2 * hd)
    b_heads = d // hd
    qscale = (hd ** -0.5) * math.log2(math.e)
    topk = min(TOPK_MAX, s // 4)
    tn = 512

    cos, sin, ci, sa, sb = _rope_tables(s)
    mod = _ada_mod(c, ada_w, ada_b)
    wg = ffn_w_gate.astype(BF16)
    wu = ffn_w_up.astype(BF16)
    wd = ffn_w_down.astype(BF16)

    x2 = x.reshape(b * s, d)
    for i in range(depth):
        j = i // 2
        x2 = _ffn(x2, mod[i], wg, wu, wd, i, 0, s)
        if i % 2 == 0:
            lam_init = 0.8 - 0.6 * math.exp(-0.3 * i)
            reps = tn // hd
            gains = jnp.concatenate([
                jnp.tile(jnp.tile(a_q_gain[j] * qscale, reps)[None], (d // tn, 1)),
                jnp.tile(jnp.tile(a_k_gain[j], reps)[None], (d // tn, 1)),
                jnp.ones((d // tn, tn), F32),
            ], axis=0)[:, None, :]
            p = _proj_a(x2, mod[i], a_w_in[j].astype(BF16), gains, cos, sin, s)
            lam_vecs = jnp.stack([a_lambda_q1[j], a_lambda_k1[j], a_lambda_q2[j], a_lambda_k2[j]])
            gain_eff = (a_subln_gain[j] * (1.0 - lam_init))[None, :]
            o = _attn_a(p.reshape(b, s, 3 * d), lam_vecs, gain_eff, a_heads, lam_init)
            x2 = _out_proj(o.reshape(b * s, d), a_w_out[j].astype(BF16), x2, mod[i], s)
        else:
            w = b_w_in[j]
            q_w = b_heads * hd
            qi_w = IDX_HEADS * hd
            o_k, o_v, o_qi, o_ki, o_wi = q_w, q_w + hd, q_w + 2 * hd, q_w + 2 * hd + qi_w, q_w + 3 * hd + qi_w
            w_re = jnp.concatenate([
                w[:, o_qi:o_qi + qi_w], w[:, :q_w], w[:, o_k:o_k + hd], w[:, o_ki:o_ki + hd],
                w[:, o_v:o_v + hd], w[:, o_wi:o_wi + IDX_HEADS],
                jnp.zeros((d, hd - IDX_HEADS), F32),
            ], axis=1).astype(BF16)
            n_tiles = w_re.shape[1] // tn
            reps = tn // hd
            g0 = jnp.zeros((n_tiles, 2, tn), F32)
            g0 = g0.at[qi_w // tn:(qi_w + q_w) // tn, 0, :].set(jnp.tile(b_q_gain[j] * qscale, reps)[None])
            g0 = g0.at[n_tiles - 1, 0, 0:hd].set(b_k_gain[j])
            g0 = g0.at[n_tiles - 1, 0, hd:2 * hd].set(b_kidx_gain[j])
            g0 = g0.at[n_tiles - 1, 1, hd:2 * hd].set(b_kidx_bias[j])
            p, wi = _proj_b(x2, mod[i], w_re, g0, cos, sin, ci, sa, sb, s)
            p3 = p.reshape(b, s, w_re.shape[1])
            v_t = jnp.swapaxes(p3[:, :, qi_w + q_w + 2 * hd:qi_w + q_w + 3 * hd], 1, 2)
            v1t = jnp.concatenate([v_t, jnp.ones_like(v_t)], axis=1)
            wi_t = jnp.swapaxes(wi.reshape(b, s, LANES), 1, 2)
            o = _attn_b(p3, wi_t, v1t, b_heads, topk)
            x2 = _out_proj(o.reshape(b * s, d), b_w_out[j].astype(BF16), x2, mod[i], s)
        x2 = _ffn(x2, mod[i], wg, wu, wd, i, 1, s)
    return x2.reshape(b, s, d)
```

```python
import functools
import math

import jax
import jax.numpy as jnp
import numpy as np
from jax import lax
from jax.experimental import pallas as pl
from jax.experimental.pallas import tpu as pltpu

ROPE_THETA = 10000.0
NORM_EPS = 1e-6
N_ADA = 9
HEAD_DIM = 128
IDX_HEADS = 16
IDX_ROPE_DIM = 64
TOPK_MAX = 256
LANES = 128
ACC_CHAINS = 64
V7X_VMEM_LIMIT = 56 * 1024 * 1024
NEG = -1e30
INT_MIN = np.int32(-(2 ** 31))

BF16 = jnp.bfloat16
F32 = jnp.float32


def _cparams(sem):
    return pltpu.CompilerParams(dimension_semantics=sem, vmem_limit_bytes=V7X_VMEM_LIMIT)


def _dot(a, b):
    return jnp.dot(a, b, preferred_element_type=F32)


def _dot_nt(a, b):
    return lax.dot_general(a, b, (((1,), (1,)), ((), ())), preferred_element_type=F32)


def _tile_lanes(x, n):
    return x if n == 1 else jnp.concatenate([x] * n, axis=1)


def _modulate(x, shift, scale):
    ms = jnp.mean(x * x, axis=-1, keepdims=True)
    return x * lax.rsqrt(ms + NORM_EPS) * (1.0 + scale) + shift


def _rms(x):
    return x * lax.rsqrt(jnp.mean(x * x, axis=-1, keepdims=True) + NORM_EPS)


def _ada_kernel(c_ref, w_ref, b_ref, o_ref):
    c = c_ref[...]
    ca = (c * (1.0 / (1.0 + jnp.exp(-c)))).astype(BF16)
    o_ref[0] = _dot(ca, w_ref[0].astype(BF16)) + b_ref[0]


def _ada_mod(c, ada_w, ada_b):
    depth, d, n = ada_w.shape
    b = c.shape[0]
    rows = 8
    c8 = jnp.zeros((rows, d), F32).at[:b].set(c)
    tn = math.gcd(n, 1024)
    out = pl.pallas_call(
        _ada_kernel,
        grid=(depth, n // tn),
        in_specs=[
            pl.BlockSpec((rows, d), lambda i, j: (0, 0)),
            pl.BlockSpec((1, d, tn), lambda i, j: (i, 0, j)),
            pl.BlockSpec((1, 1, tn), lambda i, j: (i, 0, j)),
        ],
        out_specs=pl.BlockSpec((1, rows, tn), lambda i, j: (i, 0, j)),
        out_shape=jax.ShapeDtypeStruct((depth, rows, n), F32),
        compiler_params=_cparams(("arbitrary", "arbitrary")),
        name="ada_mod",
    )(c8, ada_w, ada_b.reshape(depth, 1, n))
    return out[:, :b].reshape(depth, b, N_ADA, d)


def _ffn_kernel(x_ref, mod_ref, wg_ref, wu_ref, wd_ref, o_ref, h_ref, *, row0, nf):
    f = pl.program_id(1)

    def down_of_tile():
        h = h_ref[...]
        a = _dot(h, wg_ref[...])
        u = _dot(h, wu_ref[...])
        act = (a * (1.0 / (1.0 + jnp.exp(-a))) * u).astype(BF16)
        return _dot(act, wd_ref[...])

    assert nf >= 2

    @pl.when(f == 0)
    def _():
        h = _modulate(x_ref[...], mod_ref[0, row0:row0 + 1, :], mod_ref[0, row0 + 1:row0 + 2, :])
        h_ref[...] = h.astype(BF16)
        o_ref[...] = down_of_tile()

    @pl.when(jnp.logical_and(f > 0, f < nf - 1))
    def _():
        o_ref[...] += down_of_tile()

    @pl.when(f == nf - 1)
    def _():
        g = mod_ref[0, row0 + 2:row0 + 3, :]
        o_ref[...] = x_ref[...] + 0.5 * g * (o_ref[...] + down_of_tile())


def _ffn(x2, mod, wg, wu, wd, layer, which, seq):
    t, d = x2.shape
    f_dim = wg.shape[3]
    tm = min(512, seq)
    tf = 512 if f_dim % 512 == 0 else f_dim
    nsb = seq // tm
    nf = f_dim // tf
    row0 = 6 * which
    return pl.pallas_call(
        functools.partial(_ffn_kernel, row0=row0, nf=nf),
        grid=(t // tm, nf),
        in_specs=[
            pl.BlockSpec((tm, d), lambda i, f: (i, 0)),
            pl.BlockSpec((1, N_ADA, d), lambda i, f: (i // nsb, 0, 0)),
            pl.BlockSpec((None, None, d, tf), lambda i, f: (layer, which, 0, f)),
            pl.BlockSpec((None, None, d, tf), lambda i, f: (layer, which, 0, f)),
            pl.BlockSpec((None, None, tf, d), lambda i, f: (layer, which, f, 0)),
        ],
        out_specs=pl.BlockSpec((tm, d), lambda i, f: (i, 0)),
        out_shape=jax.ShapeDtypeStruct((t, d), F32),
        scratch_shapes=[pltpu.VMEM((tm, d), BF16)],
        compiler_params=_cparams(("arbitrary", "arbitrary")),
        name="ffn",
    )(x2, mod, wg, wu, wd)


def _rope_full(r, cos, sin):
    return r * cos + pltpu.roll(r, HEAD_DIM // 2, 1) * sin


def _rope_partial(r, ci, sa, sb):
    q = IDX_ROPE_DIM // 2
    return r * ci + pltpu.roll(r, LANES - q, 1) * sa + pltpu.roll(r, q, 1) * sb


def _proj_a_kernel(x_ref, mod_ref, w_ref, gain_ref, cos_ref, sin_ref, o_ref, h_ref, *, n_qk, tn):
    j = pl.program_id(1)

    @pl.when(j == 0)
    def _():
        h = _modulate(x_ref[...], mod_ref[0, 3:4, :], mod_ref[0, 4:5, :])
        h_ref[...] = h.astype(BF16)

    p = _dot(h_ref[...], w_ref[...])

    @pl.when(j < n_qk)
    def _():
        cos = cos_ref[...]
        sin = sin_ref[...]
        for c in range(tn // LANES):
            sl = slice(c * LANES, (c + 1) * LANES)
            r = _rms(p[:, sl]) * gain_ref[0, :, sl]
            o_ref[:, sl] = _rope_full(r, cos, sin).astype(BF16)

    @pl.when(j >= n_qk)
    def _():
        o_ref[...] = p.astype(BF16)


def _proj_a(x2, mod, w, gains, cos, sin, seq):
    t, d = x2.shape
    n = w.shape[1]
    tm = min(512, seq)
    tn = 512
    nsb = seq // tm
    n_qk = 2 * d // tn
    return pl.pallas_call(
        functools.partial(_proj_a_kernel, n_qk=n_qk, tn=tn),
        grid=(t // tm, n // tn),
        in_specs=[
            pl.BlockSpec((tm, d), lambda i, j: (i, 0)),
            pl.BlockSpec((1, N_ADA, d), lambda i, j: (i // nsb, 0, 0)),
            pl.BlockSpec((d, tn), lambda i, j: (0, j)),
            pl.BlockSpec((1, 1, tn), lambda i, j: (j, 0, 0)),
            pl.BlockSpec((tm, LANES), lambda i, j: (i % nsb, 0)),
            pl.BlockSpec((tm, LANES), lambda i, j: (i % nsb, 0)),
        ],
        out_specs=pl.BlockSpec((tm, tn), lambda i, j: (i, j)),
        out_shape=jax.ShapeDtypeStruct((t, n), BF16),
        scratch_shapes=[pltpu.VMEM((tm, d), BF16)],
        compiler_params=_cparams(("arbitrary", "arbitrary")),
        name="proj_a",
    )(x2, mod, w, gains, cos, sin)


def _proj_b_kernel(x_ref, mod_ref, w_ref, gain_ref, cos_ref, sin_ref, ci_ref, sa_ref, sb_ref,
                   o_ref, wi_ref, h_ref, *, n_q, n_qi, tn, wi_scale):
    j = pl.program_id(1)

    @pl.when(j == 0)
    def _():
        h = _modulate(x_ref[...], mod_ref[0, 3:4, :], mod_ref[0, 4:5, :])
        h_ref[...] = h.astype(BF16)

    p = _dot(h_ref[...], w_ref[...])

    @pl.when(j < n_qi)
    def _():
        ci = ci_ref[...]
        sa = sa_ref[...]
        sb = sb_ref[...]
        for c in range(tn // LANES):
            sl = slice(c * LANES, (c + 1) * LANES)
            o_ref[:, sl] = _rope_partial(p[:, sl], ci, sa, sb).astype(BF16)

    @pl.when(jnp.logical_and(j >= n_qi, j < n_q + n_qi))
    def _():
        cos = cos_ref[...]
        sin = sin_ref[...]
        for c in range(tn // LANES):
            sl = slice(c * LANES, (c + 1) * LANES)
            r = _rms(p[:, sl]) * gain_ref[0, 0:1, sl]
            o_ref[:, sl] = _rope_full(r, cos, sin).astype(BF16)

    @pl.when(j == n_q + n_qi)
    def _():
        k = _rms(p[:, 0:LANES]) * gain_ref[0, 0:1, 0:LANES]
        o_ref[:, 0:LANES] = _rope_full(k, cos_ref[...], sin_ref[...]).astype(BF16)
        ki = p[:, LANES:2 * LANES]
        mu = jnp.mean(ki, axis=-1, keepdims=True)
        kc = ki - mu
        var = jnp.mean(kc * kc, axis=-1, keepdims=True)
        kn = kc * lax.rsqrt(var + NORM_EPS) * gain_ref[0, 0:1, LANES:2 * LANES] \
            + gain_ref[0, 1:2, LANES:2 * LANES]
        o_ref[:, LANES:2 * LANES] = _rope_partial(kn, ci_ref[...], sa_ref[...], sb_ref[...]).astype(BF16)
        o_ref[:, 2 * LANES:3 * LANES] = p[:, 2 * LANES:3 * LANES].astype(BF16)
        wi = p[:, 3 * LANES:4 * LANES] * wi_scale
        o_ref[:, 3 * LANES:4 * LANES] = wi.astype(BF16)
        wi_ref[...] = wi


def _proj_b(x2, mod, w, gains, cos, sin, ci, sa, sb, seq):
    t, d = x2.shape
    n = w.shape[1]
    tm = min(512, seq)
    tn = 512
    nsb = seq // tm
    n_q = d // tn
    n_qi = IDX_HEADS * HEAD_DIM // tn
    assert n == (n_q + n_qi + 1) * tn
    wi_scale = (IDX_HEADS ** -0.5) * (HEAD_DIM ** -0.5)
    tab = pl.BlockSpec((tm, LANES), lambda i, j: (i % nsb, 0))
    return pl.pallas_call(
        functools.partial(_proj_b_kernel, n_q=n_q, n_qi=n_qi, tn=tn, wi_scale=wi_scale),
        grid=(t // tm, n // tn),
        in_specs=[
            pl.BlockSpec((tm, d), lambda i, j: (i, 0)),
            pl.BlockSpec((1, N_ADA, d), lambda i, j: (i // nsb, 0, 0)),
            pl.BlockSpec((d, tn), lambda i, j: (0, j)),
            pl.BlockSpec((1, 2, tn), lambda i, j: (j, 0, 0)),
            tab, tab, tab, tab, tab,
        ],
        out_specs=[
            pl.BlockSpec((tm, tn), lambda i, j: (i, j)),
            pl.BlockSpec((tm, LANES), lambda i, j: (i, 0)),
        ],
        out_shape=[
            jax.ShapeDtypeStruct((t, n), BF16),
            jax.ShapeDtypeStruct((t, LANES), F32),
        ],
        scratch_shapes=[pltpu.VMEM((tm, d), BF16)],
        compiler_params=_cparams(("arbitrary", "arbitrary")),
        name="proj_b",
    )(x2, mod, w, gains, cos, sin, ci, sa, sb)


def _out_kernel(o_ref, w_ref, x_ref, mod_ref, y_ref):
    y = _dot(o_ref[...], w_ref[...])
    y_ref[...] = x_ref[...] + mod_ref[0, 5:6, :] * y


def _out_proj(o2, w, x2, mod, seq):
    t, d = x2.shape
    k = o2.shape[1]
    tm = min(512, seq)
    nsb = seq // tm
    return pl.pallas_call(
        _out_kernel,
        grid=(t // tm,),
        in_specs=[
            pl.BlockSpec((tm, k), lambda i: (i, 0)),
            pl.BlockSpec((k, d), lambda i: (0, 0)),
            pl.BlockSpec((tm, d), lambda i: (i, 0)),
            pl.BlockSpec((1, N_ADA, d), lambda i: (i // nsb, 0, 0)),
        ],
        out_specs=pl.BlockSpec((tm, d), lambda i: (i, 0)),
        out_shape=jax.ShapeDtypeStruct((t, d), F32),
        compiler_params=_cparams(("arbitrary",)),
        name="out_proj",
    )(o2, w, x2, mod)


def _softmax_step(s, m_ref, l_ref, acc_ref, v, idx, rows=slice(None)):
    tk = s.shape[1]
    dv = v.shape[1]
    m_prev = m_ref[idx, rows, :]
    m_new = jnp.maximum(m_prev, jnp.max(s, axis=1, keepdims=True))
    alpha = jnp.exp2(m_prev - m_new)
    p = jnp.exp2(s - _tile_lanes(m_new, tk // LANES))
    l_ref[idx, rows, :] = alpha * l_ref[idx, rows, :] + jnp.sum(p, axis=1, keepdims=True)
    acc_ref[idx, rows, :] = _tile_lanes(alpha, dv // LANES) * acc_ref[idx, rows, :] + _dot(p.astype(BF16), v)
    m_ref[idx, rows, :] = m_new


def _attn_a_kernel(q0_ref, q1_ref, k0_ref, k1_ref, v_ref, lam_ref, gain_ref, o_ref,
                   m_ref, l_ref, acc_ref, *, tq, lam_init):
    qi = pl.program_id(2)
    m_ref[...] = jnp.full_like(m_ref, NEG)
    l_ref[...] = jnp.zeros_like(l_ref)
    acc_ref[...] = jnp.zeros_like(acc_ref)
    tk = tq
    q0 = q0_ref[0]
    q1 = q1_ref[0]

    def body(c, carry):
        off = pl.multiple_of(c * tk, tk)
        v = v_ref[0, pl.ds(off, tk), :]
        s0 = _dot_nt(q0, k0_ref[0, pl.ds(off, tk), :])
        s1 = _dot_nt(q1, k1_ref[0, pl.ds(off, tk), :])
        _softmax_step(s0, m_ref, l_ref, acc_ref, v, 0)
        _softmax_step(s1, m_ref, l_ref, acc_ref, v, 1)
        return carry

    lax.fori_loop(0, qi, body, 0)

    off = pl.multiple_of(qi * tk, tk)
    band = tq // 2
    for r in range(2):
        rows = slice(r * band, (r + 1) * band)
        width = (r + 1) * band
        v = v_ref[0, pl.ds(off, width), :]
        row = r * band + lax.broadcasted_iota(jnp.int32, (band, width), 0)
        keep = lax.broadcasted_iota(jnp.int32, (band, width), 1) <= row
        s0 = _dot_nt(q0_ref[0, rows, :], k0_ref[0, pl.ds(off, width), :])
        s1 = _dot_nt(q1_ref[0, rows, :], k1_ref[0, pl.ds(off, width), :])
        _softmax_step(jnp.where(keep, s0, NEG), m_ref, l_ref, acc_ref, v, 0, rows)
        _softmax_step(jnp.where(keep, s1, NEG), m_ref, l_ref, acc_ref, v, 1, rows)

    dv = acc_ref.shape[2]
    lam_q = lam_ref[...]
    s1 = jnp.sum(lam_q[0:1, :] * lam_q[1:2, :], axis=1, keepdims=True)
    s2 = jnp.sum(lam_q[2:3, :] * lam_q[3:4, :], axis=1, keepdims=True)
    lam = jnp.exp(s1) - jnp.exp(s2) + lam_init
    o0 = acc_ref[0] * _tile_lanes(1.0 / l_ref[0], dv // LANES)
    o1 = acc_ref[1] * _tile_lanes(1.0 / l_ref[1], dv // LANES)
    o = o0 - lam * o1
    o_ref[0] = (_rms(o) * gain_ref[...]).astype(BF16)


def _attn_a(p3, lam_vecs, gain_eff, n_heads, lam_init):
    b, s, n = p3.shape
    d = n // 3
    tq = min(1024, s)
    hd = HEAD_DIM
    kcol = d // hd
    vcol = 2 * d // (2 * hd)
    return pl.pallas_call(
        functools.partial(_attn_a_kernel, tq=tq, lam_init=lam_init),
        grid=(b, n_heads, s // tq),
        in_specs=[
            pl.BlockSpec((1, tq, hd), lambda bi, h, i: (bi, i, 2 * h)),
            pl.BlockSpec((1, tq, hd), lambda bi, h, i: (bi, i, 2 * h + 1)),
            pl.BlockSpec((1, s, hd), lambda bi, h, i: (bi, 0, kcol + 2 * h)),
            pl.BlockSpec((1, s, hd), lambda bi, h, i: (bi, 0, kcol + 2 * h + 1)),
            pl.BlockSpec((1, s, 2 * hd), lambda bi, h, i: (bi, 0, vcol + h)),
            pl.BlockSpec((4, hd), lambda bi, h, i: (0, 0)),
            pl.BlockSpec((1, 2 * hd), lambda bi, h, i: (0, 0)),
        ],
        out_specs=pl.BlockSpec((1, tq, 2 * hd), lambda bi, h, i: (bi, i, h)),
        out_shape=jax.ShapeDtypeStruct((b, s, d), BF16),
        scratch_shapes=[
            pltpu.VMEM((2, tq, LANES), F32),
            pltpu.VMEM((2, tq, LANES), F32),
            pltpu.VMEM((2, tq, 2 * hd), F32),
        ],
        compiler_params=_cparams(("arbitrary", "arbitrary", "arbitrary")),
        name="attn_a",
    )(p3, p3, p3, p3, p3, lam_vecs, gain_eff)


def _attn_b_kernel(q_ref, qi_ref, wi_ref, k_ref, ki_ref, vt_ref, o_ref,
                   keys_ref, m_ref, acc_ref, *, tq, tk, n_heads, topk, idx_bits):
    i = pl.program_id(1)
    nkc = ((i + 1) * tq + tk - 1) // tk
    kf = float(topk)

    def key_pos(off, rows):
        return off + lax.broadcasted_iota(jnp.int32, (rows, tq), 0)

    def score_chunk(c, carry):
        off = pl.multiple_of(c * tk, tk)
        kic = ki_ref[0, pl.ds(off, tk), :]
        sc = jnp.zeros((tk, tq), F32)
        for h in range(IDX_HEADS):
            s = _dot_nt(kic, qi_ref[0, :, h * HEAD_DIM:(h + 1) * HEAD_DIM])
            sc = sc + wi_ref[0, h:h + 1, :] * jnp.maximum(s, 0.0)
        bits = pltpu.bitcast(sc, jnp.int32)
        key = bits ^ ((bits >> 31) & jnp.int32(0x7FFFFFFF))
        qpos = i * tq + lax.broadcasted_iota(jnp.int32, (tk, tq), 1)
        keys_ref[pl.ds(off, tk), :] = jnp.where(key_pos(off, tk) <= qpos, key, INT_MIN)
        return carry

    lax.fori_loop(0, nkc, score_chunk, 0)

    def col_total(part):
        return jnp.sum(part.astype(F32), axis=0, keepdims=True)

    def fold_chunk(off, part, hit_fn):
        for r in range(tk // ACC_CHAINS):
            ro = off + r * ACC_CHAINS
            part = part + jnp.where(hit_fn(keys_ref[pl.ds(ro, ACC_CHAINS), :], ro), 1, 0)
        return part

    def count_ge(thr):
        def cbody(c, part):
            off = pl.multiple_of(c * tk, tk)
            return fold_chunk(off, part, lambda kk, ro: kk >= thr)
        return col_total(lax.fori_loop(0, nkc, cbody, jnp.zeros((ACC_CHAINS, tq), jnp.int32)))

    def search(it, thr):
        cand = thr + jnp.left_shift(jnp.int32(1), 31 - it)
        return jnp.where(count_ge(cand) >= kf, cand, thr)

    thr = lax.fori_loop(0, 32, search, jnp.full((1, tq), INT_MIN, jnp.int32))

    tie = jnp.where(jnp.logical_and(count_ge(thr) > kf, thr > INT_MIN), 1.0, 0.0)
    any_tie = jnp.max(tie)

    @pl.when(any_tie > 0.0)
    def _():
        need = kf - count_ge(thr + 1)

        def count_eq_before(jcut):
            def cbody(c, part):
                off = pl.multiple_of(c * tk, tk)
                return fold_chunk(off, part, lambda kk, ro: jnp.logical_and(
                    kk == thr, key_pos(ro, ACC_CHAINS) < jcut))
            return col_total(lax.fori_loop(0, nkc, cbody, jnp.zeros((ACC_CHAINS, tq), jnp.int32)))

        def jsearch(it, jcut):
            cand = jcut + jnp.left_shift(jnp.int32(1), idx_bits - 1 - it)
            return jnp.where(count_eq_before(cand) <= need, cand, jcut)

        jcut = lax.fori_loop(0, idx_bits, jsearch, jnp.zeros((1, tq), jnp.int32))

        def demote(c, carry):
            off = pl.multiple_of(c * tk, tk)
            kk = keys_ref[pl.ds(off, tk), :]
            drop = jnp.logical_and(jnp.logical_and(kk == thr, key_pos(off, tk) >= jcut), tie > 0.0)
            keys_ref[pl.ds(off, tk), :] = jnp.where(drop, kk - 1, kk)
            return carry

        lax.fori_loop(0, nkc, demote, 0)

    thr_sel = jnp.maximum(thr, INT_MIN + 1)
    m_ref[...] = jnp.full_like(m_ref, NEG)
    acc_ref[...] = jnp.zeros_like(acc_ref)

    def attend(c):
        off = pl.multiple_of(c * tk, tk)
        kc = k_ref[0, pl.ds(off, tk), :]
        vt = vt_ref[0, :, pl.ds(off, tk)]
        bias = jnp.where(keys_ref[pl.ds(off, tk), :] >= thr_sel, 0.0, NEG)
        for h in range(n_heads):
            s = _dot_nt(kc, q_ref[0, :, h * HEAD_DIM:(h + 1) * HEAD_DIM]) + bias
            m_prev = m_ref[h]
            m_new = jnp.maximum(m_prev, jnp.max(s, axis=0, keepdims=True))
            alpha = jnp.exp2(m_prev - m_new)
            p = jnp.exp2(s - m_new)
            acc_ref[h] = alpha * acc_ref[h] + _dot(vt, p.astype(BF16))
            m_ref[h] = m_new

    def attend_pair(j, carry):
        attend(2 * j)
        attend(2 * j + 1)
        return carry

    lax.fori_loop(0, nkc // 2, attend_pair, 0)

    @pl.when(nkc % 2 == 1)
    def _():
        attend(nkc - 1)

    for h in range(n_heads):
        a = acc_ref[h]
        o = a[0:HEAD_DIM, :] * (1.0 / a[HEAD_DIM:HEAD_DIM + 1, :])
        o_ref[0, :, h * HEAD_DIM:(h + 1) * HEAD_DIM] = o.T.astype(BF16)


def _attn_b(p3, wi_t, v1t, n_heads, topk):
    b, s, n = p3.shape
    tq = min(256, s)
    tk = min(512, s)
    d = n_heads * HEAD_DIM
    qi_w = IDX_HEADS * HEAD_DIM
    assert qi_w % d == 0
    kcol = (d + qi_w) // HEAD_DIM
    vrows = v1t.shape[1]
    assert vrows == 2 * HEAD_DIM
    return pl.pallas_call(
        functools.partial(_attn_b_kernel, tq=tq, tk=tk, n_heads=n_heads, topk=topk,
                          idx_bits=int(s).bit_length()),
        grid=(b, s // tq),
        in_specs=[
            pl.BlockSpec((1, tq, d), lambda bi, i: (bi, i, qi_w // d)),
            pl.BlockSpec((1, tq, qi_w), lambda bi, i: (bi, i, 0)),
            pl.BlockSpec((1, LANES, tq), lambda bi, i: (bi, 0, i)),
            pl.BlockSpec((1, s, HEAD_DIM), lambda bi, i: (bi, 0, kcol)),
            pl.BlockSpec((1, s, HEAD_DIM), lambda bi, i: (bi, 0, kcol + 1)),
            pl.BlockSpec((1, vrows, s), lambda bi, i: (bi, 0, 0)),
        ],
        out_specs=pl.BlockSpec((1, tq, d), lambda bi, i: (bi, i, 0)),
        out_shape=jax.ShapeDtypeStruct((b, s, d), BF16),
        scratch_shapes=[
            pltpu.VMEM((s, tq), jnp.int32),
            pltpu.VMEM((n_heads, 1, tq), F32),
            pltpu.VMEM((n_heads, vrows, tq), F32),
        ],
        compiler_params=_cparams(("arbitrary", "arbitrary")),
        name="attn_b",
    )(p3, p3, wi_t, p3, p3, v1t)


def _rope_tables(seq):
    pos = jnp.arange(seq, dtype=F32)[:, None]

    def cs(dim):
        inv = 1.0 / (ROPE_THETA ** (jnp.arange(0, dim, 2, dtype=F32) / dim))
        ang = pos * inv[None, :]
        return jnp.cos(ang), jnp.sin(ang)

    c, s = cs(HEAD_DIM)
    cos = jnp.concatenate([c, c], axis=1)
    sin = jnp.concatenate([-s, s], axis=1)
    c, s = cs(IDX_ROPE_DIM)
    one = jnp.ones((seq, LANES - IDX_ROPE_DIM), F32)
    z32 = jnp.zeros_like(s)
    z64 = jnp.zeros_like(one)
    ci = jnp.concatenate([c, c, one], axis=1)
    sa = jnp.concatenate([-s, z32, z64], axis=1)
    sb = jnp.concatenate([z32, s, z64], axis=1)
    return cos, sin, ci, sa, sb


def kernel(x, c, ada_w, ada_b, ffn_w_gate, ffn_w_up, ffn_w_down, a_w_in, a_w_out, a_q_gain, a_k_gain,
           a_lambda_q1, a_lambda_k1, a_lambda_q2, a_lambda_k2, a_subln_gain, b_w_in, b_w_out, b_q_gain,
           b_k_gain, b_kidx_gain, b_kidx_bias):
    b, s, d = x.shape
    depth = ada_w.shape[0]
    hd = HEAD_DIM
    a_heads = d // (2 * hd)
    b_heads = d // hd
    qscale = (hd ** -0.5) * math.log2(math.e)
    topk = min(TOPK_MAX, s // 4)
    tn = 512

    cos, sin, ci, sa, sb = _rope_tables(s)
    mod = _ada_mod(c, ada_w, ada_b)
    wg = ffn_w_gate.astype(BF16)
    wu = ffn_w_up.astype(BF16)
    wd = ffn_w_down.astype(BF16)

    x2 = x.reshape(b * s, d)
    for i in range(depth):
        j = i // 2
        x2 = _ffn(x2, mod[i], wg, wu, wd, i, 0, s)
        if i % 2 == 0:
            lam_init = 0.8 - 0.6 * math.exp(-0.3 * i)
            reps = tn // hd
            gains = jnp.concatenate([
                jnp.tile(jnp.tile(a_q_gain[j] * qscale, reps)[None], (d // tn, 1)),
                jnp.tile(jnp.tile(a_k_gain[j], reps)[None], (d // tn, 1)),
                jnp.ones((d // tn, tn), F32),
            ], axis=0)[:, None, :]
            p = _proj_a(x2, mod[i], a_w_in[j].astype(BF16), gains, cos, sin, s)
            lam_vecs = jnp.stack([a_lambda_q1[j], a_lambda_k1[j], a_lambda_q2[j], a_lambda_k2[j]])
            gain_eff = (a_subln_gain[j] * (1.0 - lam_init))[None, :]
            o = _attn_a(p.reshape(b, s, 3 * d), lam_vecs, gain_eff, a_heads, lam_init)
            x2 = _out_proj(o.reshape(b * s, d), a_w_out[j].astype(BF16), x2, mod[i], s)
        else:
            w = b_w_in[j]
            q_w = b_heads * hd
            qi_w = IDX_HEADS * hd
            o_k, o_v, o_qi, o_ki, o_wi = q_w, q_w + hd, q_w + 2 * hd, q_w + 2 * hd + qi_w, q_w + 3 * hd + qi_w
            w_re = jnp.concatenate([
                w[:, o_qi:o_qi + qi_w], w[:, :q_w], w[:, o_k:o_k + hd], w[:, o_ki:o_ki + hd],
                w[:, o_v:o_v + hd], w[:, o_wi:o_wi + IDX_HEADS],
                jnp.zeros((d, hd - IDX_HEADS), F32),
            ], axis=1).astype(BF16)
            n_tiles = w_re.shape[1] // tn
            reps = tn // hd
            g0 = jnp.zeros((n_tiles, 2, tn), F32)
            g0 = g0.at[qi_w // tn:(qi_w + q_w) // tn, 0, :].set(jnp.tile(b_q_gain[j] * qscale, reps)[None])
            g0 = g0.at[n_tiles - 1, 0, 0:hd].set(b_k_gain[j])
            g0 = g0.at[n_tiles - 1, 0, hd:2 * hd].set(b_kidx_gain[j])
            g0 = g0.at[n_tiles - 1, 1, hd:2 * hd].set(b_kidx_bias[j])
            p, wi = _proj_b(x2, mod[i], w_re, g0, cos, sin, ci, sa, sb, s)
            p3 = p.reshape(b, s, w_re.shape[1])
            v_t = jnp.swapaxes(p3[:, :, qi_w + q_w + 2 * hd:qi_w + q_w + 3 * hd], 1, 2)
            v1t = jnp.concatenate([v_t, jnp.ones_like(v_t)], axis=1)
            wi_t = jnp.swapaxes(wi.reshape(b, s, LANES), 1, 2)
            o = _attn_b(p3, wi_t, v1t, b_heads, topk)
            x2 = _out_proj(o.reshape(b * s, d), b_w_out[j].astype(BF16), x2, mod[i], s)
        x2 = _ffn(x2, mod[i], wg, wu, wd, i, 1, s)
    return x2.reshape(b, s, d)
```

```python
import functools
import math

import jax
import jax.numpy as jnp
import numpy as np
from jax import lax
from jax.experimental import pallas as pl
from jax.experimental.pallas import tpu as pltpu

ROPE_THETA = 10000.0
NORM_EPS = 1e-6
N_ADA = 9
HEAD_DIM = 128
IDX_HEADS = 16
IDX_ROPE_DIM = 64
TOPK_MAX = 256
LANES = 128
ACC_CHAINS = 64
V7X_VMEM_LIMIT = 56 * 1024 * 1024
NEG = -1e30
INT_MIN = np.int32(-(2 ** 31))

BF16 = jnp.bfloat16
F32 = jnp.float32


def _cparams(sem):
    return pltpu.CompilerParams(dimension_semantics=sem, vmem_limit_bytes=V7X_VMEM_LIMIT)


def _dot(a, b):
    return jnp.dot(a, b, preferred_element_type=F32)


def _dot_nt(a, b):
    return lax.dot_general(a, b, (((1,), (1,)), ((), ())), preferred_element_type=F32)


def _tile_lanes(x, n):
    return x if n == 1 else jnp.concatenate([x] * n, axis=1)


def _modulate(x, shift, scale):
    ms = jnp.mean(x * x, axis=-1, keepdims=True)
    return x * lax.rsqrt(ms + NORM_EPS) * (1.0 + scale) + shift


def _rms(x):
    return x * lax.rsqrt(jnp.mean(x * x, axis=-1, keepdims=True) + NORM_EPS)


def _ada_kernel(c_ref, w_ref, b_ref, o_ref):
    c = c_ref[...]
    ca = (c * (1.0 / (1.0 + jnp.exp(-c)))).astype(BF16)
    o_ref[0] = _dot(ca, w_ref[0].astype(BF16)) + b_ref[0]


def _ada_mod(c, ada_w, ada_b):
    depth, d, n = ada_w.shape
    b = c.shape[0]
    rows = 8
    c8 = jnp.zeros((rows, d), F32).at[:b].set(c)
    tn = math.gcd(n, 1024)
    out = pl.pallas_call(
        _ada_kernel,
        grid=(depth, n // tn),
        in_specs=[
            pl.BlockSpec((rows, d), lambda i, j: (0, 0)),
            pl.BlockSpec((1, d, tn), lambda i, j: (i, 0, j)),
            pl.BlockSpec((1, 1, tn), lambda i, j: (i, 0, j)),
        ],
        out_specs=pl.BlockSpec((1, rows, tn), lambda i, j: (i, 0, j)),
        out_shape=jax.ShapeDtypeStruct((depth, rows, n), F32),
        compiler_params=_cparams(("arbitrary", "arbitrary")),
        name="ada_mod",
    )(c8, ada_w, ada_b.reshape(depth, 1, n))
    return out[:, :b].reshape(depth, b, N_ADA, d)


def _ffn_kernel(x_ref, mod_ref, wg_ref, wu_ref, wd_ref, o_ref, h_ref, *, row0, nf):
    f = pl.program_id(1)

    def down_of_tile():
        h = h_ref[...]
        a = _dot(h, wg_ref[...])
        u = _dot(h, wu_ref[...])
        act = (a * (1.0 / (1.0 + jnp.exp(-a))) * u).astype(BF16)
        return _dot(act, wd_ref[...])

    assert nf >= 2

    @pl.when(f == 0)
    def _():
        h = _modulate(x_ref[...], mod_ref[0, row0:row0 + 1, :], mod_ref[0, row0 + 1:row0 + 2, :])
        h_ref[...] = h.astype(BF16)
        o_ref[...] = down_of_tile()

    @pl.when(jnp.logical_and(f > 0, f < nf - 1))
    def _():
        o_ref[...] += down_of_tile()

    @pl.when(f == nf - 1)
    def _():
        g = mod_ref[0, row0 + 2:row0 + 3, :]
        o_ref[...] = x_ref[...] + 0.5 * g * (o_ref[...] + down_of_tile())


def _ffn(x2, mod, wg, wu, wd, layer, which, seq):
    t, d = x2.shape
    f_dim = wg.shape[3]
    tm = min(512, seq)
    tf = 512 if f_dim % 512 == 0 else f_dim
    nsb = seq // tm
    nf = f_dim // tf
    row0 = 6 * which
    return pl.pallas_call(
        functools.partial(_ffn_kernel, row0=row0, nf=nf),
        grid=(t // tm, nf),
        in_specs=[
            pl.BlockSpec((tm, d), lambda i, f: (i, 0)),
            pl.BlockSpec((1, N_ADA, d), lambda i, f: (i // nsb, 0, 0)),
            pl.BlockSpec((None, None, d, tf), lambda i, f: (layer, which, 0, f)),
            pl.BlockSpec((None, None, d, tf), lambda i, f: (layer, which, 0, f)),
            pl.BlockSpec((None, None, tf, d), lambda i, f: (layer, which, f, 0)),
        ],
        out_specs=pl.BlockSpec((tm, d), lambda i, f: (i, 0)),
        out_shape=jax.ShapeDtypeStruct((t, d), F32),
        scratch_shapes=[pltpu.VMEM((tm, d), BF16)],
        compiler_params=_cparams(("arbitrary", "arbitrary")),
        name="ffn",
    )(x2, mod, wg, wu, wd)


def _rope_full(r, cos, sin):
    return r * cos + pltpu.roll(r, HEAD_DIM // 2, 1) * sin


def _rope_partial(r, ci, sa, sb):
    q = IDX_ROPE_DIM // 2
    return r * ci + pltpu.roll(r, LANES - q, 1) * sa + pltpu.roll(r, q, 1) * sb


def _proj_a_kernel(x_ref, mod_ref, w_ref, gain_ref, cos_ref, sin_ref, o_ref, h_ref, *, n_qk, tn):
    j = pl.program_id(1)

    @pl.when(j == 0)
    def _():
        h = _modulate(x_ref[...], mod_ref[0, 3:4, :], mod_ref[0, 4:5, :])
        h_ref[...] = h.astype(BF16)

    p = _dot(h_ref[...], w_ref[...])

    @pl.when(j < n_qk)
    def _():
        cos = cos_ref[...]
        sin = sin_ref[...]
        for c in range(tn // LANES):
            sl = slice(c * LANES, (c + 1) * LANES)
            r = _rms(p[:, sl]) * gain_ref[0, :, sl]
            o_ref[:, sl] = _rope_full(r, cos, sin).astype(BF16)

    @pl.when(j >= n_qk)
    def _():
        o_ref[...] = p.astype(BF16)


def _proj_a(x2, mod, w, gains, cos, sin, seq):
    t, d = x2.shape
    n = w.shape[1]
    tm = min(512, seq)
    tn = 512
    nsb = seq // tm
    n_qk = 2 * d // tn
    return pl.pallas_call(
        functools.partial(_proj_a_kernel, n_qk=n_qk, tn=tn),
        grid=(t // tm, n // tn),
        in_specs=[
            pl.BlockSpec((tm, d), lambda i, j: (i, 0)),
            pl.BlockSpec((1, N_ADA, d), lambda i, j: (i // nsb, 0, 0)),
            pl.BlockSpec((d, tn), lambda i, j: (0, j)),
            pl.BlockSpec((1, 1, tn), lambda i, j: (j, 0, 0)),
            pl.BlockSpec((tm, LANES), lambda i, j: (i % nsb, 0)),
            pl.BlockSpec((tm, LANES), lambda i, j: (i % nsb, 0)),
        ],
        out_specs=pl.BlockSpec((tm, tn), lambda i, j: (i, j)),
        out_shape=jax.ShapeDtypeStruct((t, n), BF16),
        scratch_shapes=[pltpu.VMEM((tm, d), BF16)],
        compiler_params=_cparams(("arbitrary", "arbitrary")),
        name="proj_a",
    )(x2, mod, w, gains, cos, sin)


def _proj_b_kernel(x_ref, mod_ref, w_ref, gain_ref, cos_ref, sin_ref, ci_ref, sa_ref, sb_ref,
                   o_ref, wi_ref, h_ref, *, n_q, n_qi, tn, wi_scale):
    j = pl.program_id(1)

    @pl.when(j == 0)
    def _():
        h = _modulate(x_ref[...], mod_ref[0, 3:4, :], mod_ref[0, 4:5, :])
        h_ref[...] = h.astype(BF16)

    p = _dot(h_ref[...], w_ref[...])

    @pl.when(j < n_qi)
    def _():
        ci = ci_ref[...]
        sa = sa_ref[...]
        sb = sb_ref[...]
        for c in range(tn // LANES):
            sl = slice(c * LANES, (c + 1) * LANES)
            o_ref[:, sl] = _rope_partial(p[:, sl], ci, sa, sb).astype(BF16)

    @pl.when(jnp.logical_and(j >= n_qi, j < n_q + n_qi))
    def _():
        cos = cos_ref[...]
        sin = sin_ref[...]
        for c in range(tn // LANES):
            sl = slice(c * LANES, (c + 1) * LANES)
            r = _rms(p[:, sl]) * gain_ref[0, 0:1, sl]
            o_ref[:, sl] = _rope_full(r, cos, sin).astype(BF16)

    @pl.when(j == n_q + n_qi)
    def _():
        k = _rms(p[:, 0:LANES]) * gain_ref[0, 0:1, 0:LANES]
        o_ref[:, 0:LANES] = _rope_full(k, cos_ref[...], sin_ref[...]).astype(BF16)
        ki = p[:, LANES:2 * LANES]
        mu = jnp.mean(ki, axis=-1, keepdims=True)
        kc = ki - mu
        var = jnp.mean(kc * kc, axis=-1, keepdims=True)
        kn = kc * lax.rsqrt(var + NORM_EPS) * gain_ref[0, 0:1, LANES:2 * LANES] \
            + gain_ref[0, 1:2, LANES:2 * LANES]
        o_ref[:, LANES:2 * LANES] = _rope_partial(kn, ci_ref[...], sa_ref[...], sb_ref[...]).astype(BF16)
        o_ref[:, 2 * LANES:3 * LANES] = p[:, 2 * LANES:3 * LANES].astype(BF16)
        wi = p[:, 3 * LANES:4 * LANES] * wi_scale
        o_ref[:, 3 * LANES:4 * LANES] = wi.astype(BF16)
        wi_ref[...] = wi


def _proj_b(x2, mod, w, gains, cos, sin, ci, sa, sb, seq):
    t, d = x2.shape
    n = w.shape[1]
    tm = min(512, seq)
    tn = 512
    nsb = seq // tm
    n_q = d // tn
    n_qi = IDX_HEADS * HEAD_DIM // tn
    assert n == (n_q + n_qi + 1) * tn
    wi_scale = (IDX_HEADS ** -0.5) * (HEAD_DIM ** -0.5)
    tab = pl.BlockSpec((tm, LANES), lambda i, j: (i % nsb, 0))
    return pl.pallas_call(
        functools.partial(_proj_b_kernel, n_q=n_q, n_qi=n_qi, tn=tn, wi_scale=wi_scale),
        grid=(t // tm, n // tn),
        in_specs=[
            pl.BlockSpec((tm, d), lambda i, j: (i, 0)),
            pl.BlockSpec((1, N_ADA, d), lambda i, j: (i // nsb, 0, 0)),
            pl.BlockSpec((d, tn), lambda i, j: (0, j)),
            pl.BlockSpec((1, 2, tn), lambda i, j: (j, 0, 0)),
            tab, tab, tab, tab, tab,
        ],
        out_specs=[
            pl.BlockSpec((tm, tn), lambda i, j: (i, j)),
            pl.BlockSpec((tm, LANES), lambda i, j: (i, 0)),
        ],
        out_shape=[
            jax.ShapeDtypeStruct((t, n), BF16),
            jax.ShapeDtypeStruct((t, LANES), F32),
        ],
        scratch_shapes=[pltpu.VMEM((tm, d), BF16)],
        compiler_params=_cparams(("arbitrary", "arbitrary")),
        name="proj_b",
    )(x2, mod, w, gains, cos, sin, ci, sa, sb)


def _out_kernel(o_ref, w_ref, x_ref, mod_ref, y_ref):
    y = _dot(o_ref[...], w_ref[...])
    y_ref[...] = x_ref[...] + mod_ref[0, 5:6, :] * y


def _out_proj(o2, w, x2, mod, seq):
    t, d = x2.shape
    k = o2.shape[1]
    tm = min(512, seq)
    nsb = seq // tm
    return pl.pallas_call(
        _out_kernel,
        grid=(t // tm,),
        in_specs=[
            pl.BlockSpec((tm, k), lambda i: (i, 0)),
            pl.BlockSpec((k, d), lambda i: (0, 0)),
            pl.BlockSpec((tm, d), lambda i: (i, 0)),
            pl.BlockSpec((1, N_ADA, d), lambda i: (i // nsb, 0, 0)),
        ],
        out_specs=pl.BlockSpec((tm, d), lambda i: (i, 0)),
        out_shape=jax.ShapeDtypeStruct((t, d), F32),
        compiler_params=_cparams(("arbitrary",)),
        name="out_proj",
    )(o2, w, x2, mod)


def _softmax_step(s, m_ref, l_ref, acc_ref, v, idx, rows=slice(None)):
    tk = s.shape[1]
    dv = v.shape[1]
    m_prev = m_ref[idx, rows, :]
    m_new = jnp.maximum(m_prev, jnp.max(s, axis=1, keepdims=True))
    alpha = jnp.exp2(m_prev - m_new)
    p = jnp.exp2(s - _tile_lanes(m_new, tk // LANES))
    l_ref[idx, rows, :] = alpha * l_ref[idx, rows, :] + jnp.sum(p, axis=1, keepdims=True)
    acc_ref[idx, rows, :] = _tile_lanes(alpha, dv // LANES) * acc_ref[idx, rows, :] + _dot(p.astype(BF16), v)
    m_ref[idx, rows, :] = m_new


def _attn_a_kernel(q0_ref, q1_ref, k0_ref, k1_ref, v_ref, lam_ref, gain_ref, o_ref,
                   m_ref, l_ref, acc_ref, *, tq, lam_init):
    qi = pl.program_id(2)
    m_ref[...] = jnp.full_like(m_ref, NEG)
    l_ref[...] = jnp.zeros_like(l_ref)
    acc_ref[...] = jnp.zeros_like(acc_ref)
    tk = tq
    q0 = q0_ref[0]
    q1 = q1_ref[0]

    def body(c, carry):
        off = pl.multiple_of(c * tk, tk)
        v = v_ref[0, pl.ds(off, tk), :]
        s0 = _dot_nt(q0, k0_ref[0, pl.ds(off, tk), :])
        s1 = _dot_nt(q1, k1_ref[0, pl.ds(off, tk), :])
        _softmax_step(s0, m_ref, l_ref, acc_ref, v, 0)
        _softmax_step(s1, m_ref, l_ref, acc_ref, v, 1)
        return carry

    lax.fori_loop(0, qi, body, 0)

    off = pl.multiple_of(qi * tk, tk)
    band = tq // 2
    for r in range(2):
        rows = slice(r * band, (r + 1) * band)
        width = (r + 1) * band
        v = v_ref[0, pl.ds(off, width), :]
        row = r * band + lax.broadcasted_iota(jnp.int32, (band, width), 0)
        keep = lax.broadcasted_iota(jnp.int32, (band, width), 1) <= row
        s0 = _dot_nt(q0_ref[0, rows, :], k0_ref[0, pl.ds(off, width), :])
        s1 = _dot_nt(q1_ref[0, rows, :], k1_ref[0, pl.ds(off, width), :])
        _softmax_step(jnp.where(keep, s0, NEG), m_ref, l_ref, acc_ref, v, 0, rows)
        _softmax_step(jnp.where(keep, s1, NEG), m_ref, l_ref, acc_ref, v, 1, rows)

    dv = acc_ref.shape[2]
    lam_q = lam_ref[...]
    s1 = jnp.sum(lam_q[0:1, :] * lam_q[1:2, :], axis=1, keepdims=True)
    s2 = jnp.sum(lam_q[2:3, :] * lam_q[3:4, :], axis=1, keepdims=True)
    lam = jnp.exp(s1) - jnp.exp(s2) + lam_init
    o0 = acc_ref[0] * _tile_lanes(1.0 / l_ref[0], dv // LANES)
    o1 = acc_ref[1] * _tile_lanes(1.0 / l_ref[1], dv // LANES)
    o = o0 - lam * o1
    o_ref[0] = (_rms(o) * gain_ref[...]).astype(BF16)


def _attn_a(p3, lam_vecs, gain_eff, n_heads, lam_init):
    b, s, n = p3.shape
    d = n // 3
    tq = min(1024, s)
    hd = HEAD_DIM
    kcol = d // hd
    vcol = 2 * d // (2 * hd)
    return pl.pallas_call(
        functools.partial(_attn_a_kernel, tq=tq, lam_init=lam_init),
        grid=(b, n_heads, s // tq),
        in_specs=[
            pl.BlockSpec((1, tq, hd), lambda bi, h, i: (bi, i, 2 * h)),
            pl.BlockSpec((1, tq, hd), lambda bi, h, i: (bi, i, 2 * h + 1)),
            pl.BlockSpec((1, s, hd), lambda bi, h, i: (bi, 0, kcol + 2 * h)),
            pl.BlockSpec((1, s, hd), lambda bi, h, i: (bi, 0, kcol + 2 * h + 1)),
            pl.BlockSpec((1, s, 2 * hd), lambda bi, h, i: (bi, 0, vcol + h)),
            pl.BlockSpec((4, hd), lambda bi, h, i: (0, 0)),
            pl.BlockSpec((1, 2 * hd), lambda bi, h, i: (0, 0)),
        ],
        out_specs=pl.BlockSpec((1, tq, 2 * hd), lambda bi, h, i: (bi, i, h)),
        out_shape=jax.ShapeDtypeStruct((b, s, d), BF16),
        scratch_shapes=[
            pltpu.VMEM((2, tq, LANES), F32),
            pltpu.VMEM((2, tq, LANES), F32),
            pltpu.VMEM((2, tq, 2 * hd), F32),
        ],
        compiler_params=_cparams(("arbitrary", "arbitrary", "arbitrary")),
        name="attn_a",
    )(p3, p3, p3, p3, p3, lam_vecs, gain_eff)


def _attn_b_kernel(q_ref, qi_ref, wi_ref, k_ref, ki_ref, vt_ref, o_ref,
                   keys_ref, m_ref, acc_ref, *, tq, tk, n_heads, topk, idx_bits):
    i = pl.program_id(1)
    n_keys = (i + 1) * tq
    n_full = n_keys // tk
    has_tail = n_keys % tk != 0
    nkc = (n_keys + tk - 1) // tk
    kf = float(topk)

    def key_pos(off, rows):
        return off + lax.broadcasted_iota(jnp.int32, (rows, tq), 0)

    def score_block(off, width):
        kic = ki_ref[0, pl.ds(off, width), :]
        sc = jnp.zeros((width, tq), F32)
        for h in range(IDX_HEADS):
            s = _dot_nt(kic, qi_ref[0, :, h * HEAD_DIM:(h + 1) * HEAD_DIM])
            sc = sc + wi_ref[0, h:h + 1, :] * jnp.maximum(s, 0.0)
        bits = pltpu.bitcast(sc, jnp.int32)
        key = bits ^ ((bits >> 31) & jnp.int32(0x7FFFFFFF))
        qpos = i * tq + lax.broadcasted_iota(jnp.int32, (width, tq), 1)
        keys_ref[pl.ds(off, width), :] = jnp.where(key_pos(off, width) <= qpos, key, INT_MIN)

    def score_pair(j, carry):
        score_block(pl.multiple_of(2 * j * tk, tk), tk)
        score_block(pl.multiple_of((2 * j + 1) * tk, tk), tk)
        return carry

    lax.fori_loop(0, n_full // 2, score_pair, 0)

    @pl.when(n_full % 2 == 1)
    def _():
        score_block(pl.multiple_of((n_full - 1) * tk, tk), tk)

    if tk > tq:
        @pl.when(has_tail)
        def _():
            off = pl.multiple_of(n_full * tk, tk)
            score_block(off, tq)
            keys_ref[pl.ds(off + tq, tk - tq), :] = jnp.full((tk - tq, tq), INT_MIN, jnp.int32)

    def col_total(part):
        return jnp.sum(part.astype(F32), axis=0, keepdims=True)

    def fold_chunk(off, part, hit_fn):
        for r in range(tk // ACC_CHAINS):
            ro = off + r * ACC_CHAINS
            part = part + jnp.where(hit_fn(keys_ref[pl.ds(ro, ACC_CHAINS), :], ro), 1, 0)
        return part

    def count_ge(thr):
        def cbody(c, part):
            off = pl.multiple_of(c * tk, tk)
            return fold_chunk(off, part, lambda kk, ro: kk >= thr)
        return col_total(lax.fori_loop(0, nkc, cbody, jnp.zeros((ACC_CHAINS, tq), jnp.int32)))

    def search(it, thr):
        cand = thr + jnp.left_shift(jnp.int32(1), 31 - it)
        return jnp.where(count_ge(cand) >= kf, cand, thr)

    thr = lax.fori_loop(0, 32, search, jnp.full((1, tq), INT_MIN, jnp.int32))

    tie = jnp.where(jnp.logical_and(count_ge(thr) > kf, thr > INT_MIN), 1.0, 0.0)
    any_tie = jnp.max(tie)

    @pl.when(any_tie > 0.0)
    def _():
        need = kf - count_ge(thr + 1)

        def count_eq_before(jcut):
            def cbody(c, part):
                off = pl.multiple_of(c * tk, tk)
                return fold_chunk(off, part, lambda kk, ro: jnp.logical_and(
                    kk == thr, key_pos(ro, ACC_CHAINS) < jcut))
            return col_total(lax.fori_loop(0, nkc, cbody, jnp.zeros((ACC_CHAINS, tq), jnp.int32)))

        def jsearch(it, jcut):
            cand = jcut + jnp.left_shift(jnp.int32(1), idx_bits - 1 - it)
            return jnp.where(count_eq_before(cand) <= need, cand, jcut)

        jcut = lax.fori_loop(0, idx_bits, jsearch, jnp.zeros((1, tq), jnp.int32))

        def demote(c, carry):
            off = pl.multiple_of(c * tk, tk)
            kk = keys_ref[pl.ds(off, tk), :]
            drop = jnp.logical_and(jnp.logical_and(kk == thr, key_pos(off, tk) >= jcut), tie > 0.0)
            keys_ref[pl.ds(off, tk), :] = jnp.where(drop, kk - 1, kk)
            return carry

        lax.fori_loop(0, nkc, demote, 0)

    thr_sel = jnp.maximum(thr, INT_MIN + 1)
    m_ref[...] = jnp.full_like(m_ref, NEG)
    acc_ref[...] = jnp.zeros_like(acc_ref)

    def attend(c, width=tk):
        off = pl.multiple_of(c * tk, tk)
        kc = k_ref[0, pl.ds(off, width), :]
        vt = vt_ref[0, :, pl.ds(off, width)]
        bias = jnp.where(keys_ref[pl.ds(off, width), :] >= thr_sel, 0.0, NEG)
        for h in range(n_heads):
            s = _dot_nt(kc, q_ref[0, :, h * HEAD_DIM:(h + 1) * HEAD_DIM]) + bias
            m_prev = m_ref[h]
            m_new = jnp.maximum(m_prev, jnp.max(s, axis=0, keepdims=True))
            alpha = jnp.exp2(m_prev - m_new)
            p = jnp.exp2(s - m_new)
            acc_ref[h] = alpha * acc_ref[h] + _dot(vt, p.astype(BF16))
            m_ref[h] = m_new

    def attend_quad(j, carry):
        for u in range(4):
            attend(4 * j + u)
        return carry

    lax.fori_loop(0, n_full // 4, attend_quad, 0)

    @pl.when(n_full % 4 >= 2)
    def _():
        attend((n_full // 4) * 4)
        attend((n_full // 4) * 4 + 1)

    @pl.when(n_full % 2 == 1)
    def _():
        attend(n_full - 1)

    if tk > tq:
        @pl.when(has_tail)
        def _():
            attend(n_full, tq)

    for h in range(n_heads):
        a = acc_ref[h]
        o = a[0:HEAD_DIM, :] * (1.0 / a[HEAD_DIM:HEAD_DIM + 1, :])
        o_ref[0, :, h * HEAD_DIM:(h + 1) * HEAD_DIM] = o.T.astype(BF16)


def _attn_b(p3, wi_t, v1t, n_heads, topk):
    b, s, n = p3.shape
    tq = min(256, s)
    tk = min(512, s)
    d = n_heads * HEAD_DIM
    qi_w = IDX_HEADS * HEAD_DIM
    assert qi_w % d == 0
    kcol = (d + qi_w) // HEAD_DIM
    vrows = v1t.shape[1]
    assert vrows == 2 * HEAD_DIM
    assert tk in (tq, 2 * tq)
    return pl.pallas_call(
        functools.partial(_attn_b_kernel, tq=tq, tk=tk, n_heads=n_heads, topk=topk,
                          idx_bits=int(s).bit_length()),
        grid=(b, s // tq),
        in_specs=[
            pl.BlockSpec((1, tq, d), lambda bi, i: (bi, i, qi_w // d)),
            pl.BlockSpec((1, tq, qi_w), lambda bi, i: (bi, i, 0)),
            pl.BlockSpec((1, LANES, tq), lambda bi, i: (bi, 0, i)),
            pl.BlockSpec((1, s, HEAD_DIM), lambda bi, i: (bi, 0, kcol)),
            pl.BlockSpec((1, s, HEAD_DIM), lambda bi, i: (bi, 0, kcol + 1)),
            pl.BlockSpec((1, vrows, s), lambda bi, i: (bi, 0, 0)),
        ],
        out_specs=pl.BlockSpec((1, tq, d), lambda bi, i: (bi, i, 0)),
        out_shape=jax.ShapeDtypeStruct((b, s, d), BF16),
        scratch_shapes=[
            pltpu.VMEM((s, tq), jnp.int32),
            pltpu.VMEM((n_heads, 1, tq), F32),
            pltpu.VMEM((n_heads, vrows, tq), F32),
        ],
        compiler_params=_cparams(("arbitrary", "arbitrary")),
        name="attn_b",
    )(p3, p3, wi_t, p3, p3, v1t)


def _rope_tables(seq):
    pos = jnp.arange(seq, dtype=F32)[:, None]

    def cs(dim):
        inv = 1.0 / (ROPE_THETA ** (jnp.arange(0, dim, 2, dtype=F32) / dim))
        ang = pos * inv[None, :]
        return jnp.cos(ang), jnp.sin(ang)

    c, s = cs(HEAD_DIM)
    cos = jnp.concatenate([c, c], axis=1)
    sin = jnp.concatenate([-s, s], axis=1)
    c, s = cs(IDX_ROPE_DIM)
    one = jnp.ones((seq, LANES - IDX_ROPE_DIM), F32)
    z32 = jnp.zeros_like(s)
    z64 = jnp.zeros_like(one)
    ci = jnp.concatenate([c, c, one], axis=1)
    sa = jnp.concatenate([-s, z32, z64], axis=1)
    sb = jnp.concatenate([z32, s, z64], axis=1)
    return cos, sin, ci, sa, sb


def kernel(x, c, ada_w, ada_b, ffn_w_gate, ffn_w_up, ffn_w_down, a_w_in, a_w_out, a_q_gain, a_k_gain,
           a_lambda_q1, a_lambda_k1, a_lambda_q2, a_lambda_k2, a_subln_gain, b_w_in, b_w_out, b_q_gain,
           b_k_gain, b_kidx_gain, b_kidx_bias):
    b, s, d = x.shape
    depth = ada_w.shape[0]
    hd = HEAD_DIM
    a_heads = d // (2 * hd)
    b_heads = d // hd
    qscale = (hd ** -0.5) * math.log2(math.e)
    topk = min(TOPK_MAX, s // 4)
    tn = 512

    cos, sin, ci, sa, sb = _rope_tables(s)
    mod = _ada_mod(c, ada_w, ada_b)
    wg = ffn_w_gate.astype(BF16)
    wu = ffn_w_up.astype(BF16)
    wd = ffn_w_down.astype(BF16)

    x2 = x.reshape(b * s, d)
    for i in range(depth):
        j = i // 2
        x2 = _ffn(x2, mod[i], wg, wu, wd, i, 0, s)
        if i % 2 == 0:
            lam_init = 0.8 - 0.6 * math.exp(-0.3 * i)
            reps = tn // hd
            gains = jnp.concatenate([
                jnp.tile(jnp.tile(a_q_gain[j] * qscale, reps)[None], (d // tn, 1)),
                jnp.tile(jnp.tile(a_k_gain[j], reps)[None], (d // tn, 1)),
                jnp.ones((d // tn, tn), F32),
            ], axis=0)[:, None, :]
            p = _proj_a(x2, mod[i], a_w_in[j].astype(BF16), gains, cos, sin, s)
            lam_vecs = jnp.stack([a_lambda_q1[j], a_lambda_k1[j], a_lambda_q2[j], a_lambda_k2[j]])
            gain_eff = (a_subln_gain[j] * (1.0 - lam_init))[None, :]
            o = _attn_a(p.reshape(b, s, 3 * d), lam_vecs, gain_eff, a_heads, lam_init)
            x2 = _out_proj(o.reshape(b * s, d), a_w_out[j].astype(BF16), x2, mod[i], s)
        else:
            w = b_w_in[j]
            q_w = b_heads * hd
            qi_w = IDX_HEADS * hd
            o_k, o_v, o_qi, o_ki, o_wi = q_w, q_w + hd, q_w + 2 * hd, q_w + 2 * hd + qi_w, q_w + 3 * hd + qi_w
            w_re = jnp.concatenate([
                w[:, o_qi:o_qi + qi_w], w[:, :q_w], w[:, o_k:o_k + hd], w[:, o_ki:o_ki + hd],
                w[:, o_v:o_v + hd], w[:, o_wi:o_wi + IDX_HEADS],
                jnp.zeros((d, hd - IDX_HEADS), F32),
            ], axis=1).astype(BF16)
            n_tiles = w_re.shape[1] // tn
            reps = tn // hd
            g0 = jnp.zeros((n_tiles, 2, tn), F32)
            g0 = g0.at[qi_w // tn:(qi_w + q_w) // tn, 0, :].set(jnp.tile(b_q_gain[j] * qscale, reps)[None])
            g0 = g0.at[n_tiles - 1, 0, 0:hd].set(b_k_gain[j])
            g0 = g0.at[n_tiles - 1, 0, hd:2 * hd].set(b_kidx_gain[j])
            g0 = g0.at[n_tiles - 1, 1, hd:2 * hd].set(b_kidx_bias[j])
            p, wi = _proj_b(x2, mod[i], w_re, g0, cos, sin, ci, sa, sb, s)
            p3 = p.reshape(b, s, w_re.shape[1])
            v_t = jnp.swapaxes(p3[:, :, qi_w + q_w + 2 * hd:qi_w + q_w + 3 * hd], 1, 2)
            v1t = jnp.concatenate([v_t, jnp.ones_like(v_t)], axis=1)
            wi_t = jnp.swapaxes(wi.reshape(b, s, LANES), 1, 2)
            o = _attn_b(p3, wi_t, v1t, b_heads, topk)
            x2 = _out_proj(o.reshape(b * s, d), b_w_out[j].astype(BF16), x2, mod[i], s)
        x2 = _ffn(x2, mod[i], wg, wu, wd, i, 1, s)
    return x2.reshape(b, s, d)
```

```python
import functools
import math

import jax
import jax.numpy as jnp
import numpy as np
from jax import lax
from jax.experimental import pallas as pl
from jax.experimental.pallas import tpu as pltpu

ROPE_THETA = 10000.0
NORM_EPS = 1e-6
N_ADA = 9
HEAD_DIM = 128
IDX_HEADS = 16
IDX_ROPE_DIM = 64
TOPK_MAX = 256
LANES = 128
ACC_CHAINS = 64
V7X_VMEM_LIMIT = 56 * 1024 * 1024
NEG = -1e30
INT_MIN = np.int32(-(2 ** 31))

BF16 = jnp.bfloat16
F32 = jnp.float32


def _cparams(sem):
    return pltpu.CompilerParams(dimension_semantics=sem, vmem_limit_bytes=V7X_VMEM_LIMIT)


def _dot(a, b):
    return jnp.dot(a, b, preferred_element_type=F32)


def _dot_nt(a, b):
    return lax.dot_general(a, b, (((1,), (1,)), ((), ())), preferred_element_type=F32)


def _tile_lanes(x, n):
    return x if n == 1 else jnp.concatenate([x] * n, axis=1)


def _modulate(x, shift, scale):
    ms = jnp.mean(x * x, axis=-1, keepdims=True)
    return x * lax.rsqrt(ms + NORM_EPS) * (1.0 + scale) + shift


def _rms(x):
    return x * lax.rsqrt(jnp.mean(x * x, axis=-1, keepdims=True) + NORM_EPS)


def _ada_kernel(c_ref, w_ref, b_ref, o_ref):
    c = c_ref[...]
    ca = (c * (1.0 / (1.0 + jnp.exp(-c)))).astype(BF16)
    o_ref[0] = _dot(ca, w_ref[0].astype(BF16)) + b_ref[0]


def _ada_mod(c, ada_w, ada_b):
    depth, d, n = ada_w.shape
    b = c.shape[0]
    rows = 8
    c8 = jnp.zeros((rows, d), F32).at[:b].set(c)
    tn = math.gcd(n, 1024)
    out = pl.pallas_call(
        _ada_kernel,
        grid=(depth, n // tn),
        in_specs=[
            pl.BlockSpec((rows, d), lambda i, j: (0, 0)),
            pl.BlockSpec((1, d, tn), lambda i, j: (i, 0, j)),
            pl.BlockSpec((1, 1, tn), lambda i, j: (i, 0, j)),
        ],
        out_specs=pl.BlockSpec((1, rows, tn), lambda i, j: (i, 0, j)),
        out_shape=jax.ShapeDtypeStruct((depth, rows, n), F32),
        compiler_params=_cparams(("arbitrary", "arbitrary")),
        name="ada_mod",
    )(c8, ada_w, ada_b.reshape(depth, 1, n))
    return out[:, :b].reshape(depth, b, N_ADA, d)


def _ffn_kernel(x_ref, mod_ref, wg_ref, wu_ref, wd_ref, o_ref, h_ref, *, row0, nf):
    f = pl.program_id(1)

    def down_of_tile():
        h = h_ref[...]
        a = _dot(h, wg_ref[...])
        u = _dot(h, wu_ref[...])
        act = (a * (1.0 / (1.0 + jnp.exp(-a))) * u).astype(BF16)
        return _dot(act, wd_ref[...])

    assert nf >= 2

    @pl.when(f == 0)
    def _():
        h = _modulate(x_ref[...], mod_ref[0, row0:row0 + 1, :], mod_ref[0, row0 + 1:row0 + 2, :])
        h_ref[...] = h.astype(BF16)
        o_ref[...] = down_of_tile()

    @pl.when(jnp.logical_and(f > 0, f < nf - 1))
    def _():
        o_ref[...] += down_of_tile()

    @pl.when(f == nf - 1)
    def _():
        g = mod_ref[0, row0 + 2:row0 + 3, :]
        o_ref[...] = x_ref[...] + 0.5 * g * (o_ref[...] + down_of_tile())


def _ffn(x2, mod, wg, wu, wd, layer, which, seq):
    t, d = x2.shape
    f_dim = wg.shape[3]
    tm = min(512, seq)
    tf = 512 if f_dim % 512 == 0 else f_dim
    nsb = seq // tm
    nf = f_dim // tf
    row0 = 6 * which
    return pl.pallas_call(
        functools.partial(_ffn_kernel, row0=row0, nf=nf),
        grid=(t // tm, nf),
        in_specs=[
            pl.BlockSpec((tm, d), lambda i, f: (i, 0)),
            pl.BlockSpec((1, N_ADA, d), lambda i, f: (i // nsb, 0, 0)),
            pl.BlockSpec((None, None, d, tf), lambda i, f: (layer, which, 0, f)),
            pl.BlockSpec((None, None, d, tf), lambda i, f: (layer, which, 0, f)),
            pl.BlockSpec((None, None, tf, d), lambda i, f: (layer, which, f, 0)),
        ],
        out_specs=pl.BlockSpec((tm, d), lambda i, f: (i, 0)),
        out_shape=jax.ShapeDtypeStruct((t, d), F32),
        scratch_shapes=[pltpu.VMEM((tm, d), BF16)],
        compiler_params=_cparams(("arbitrary", "arbitrary")),
        name="ffn",
    )(x2, mod, wg, wu, wd)


def _rope_full(r, cos, sin):
    return r * cos + pltpu.roll(r, HEAD_DIM // 2, 1) * sin


def _rope_partial(r, ci, sa, sb):
    q = IDX_ROPE_DIM // 2
    return r * ci + pltpu.roll(r, LANES - q, 1) * sa + pltpu.roll(r, q, 1) * sb


def _proj_a_kernel(x_ref, mod_ref, w_ref, gain_ref, cos_ref, sin_ref, o_ref, h_ref, *, n_qk, tn):
    j = pl.program_id(1)

    @pl.when(j == 0)
    def _():
        h = _modulate(x_ref[...], mod_ref[0, 3:4, :], mod_ref[0, 4:5, :])
        h_ref[...] = h.astype(BF16)

    p = _dot(h_ref[...], w_ref[...])

    @pl.when(j < n_qk)
    def _():
        cos = cos_ref[...]
        sin = sin_ref[...]
        for c in range(tn // LANES):
            sl = slice(c * LANES, (c + 1) * LANES)
            r = _rms(p[:, sl]) * gain_ref[0, :, sl]
            o_ref[:, sl] = _rope_full(r, cos, sin).astype(BF16)

    @pl.when(j >= n_qk)
    def _():
        o_ref[...] = p.astype(BF16)


def _proj_a(x2, mod, w, gains, cos, sin, seq):
    t, d = x2.shape
    n = w.shape[1]
    tm = min(1024, seq)
    tn = 512
    nsb = seq // tm
    n_qk = 2 * d // tn
    return pl.pallas_call(
        functools.partial(_proj_a_kernel, n_qk=n_qk, tn=tn),
        grid=(t // tm, n // tn),
        in_specs=[
            pl.BlockSpec((tm, d), lambda i, j: (i, 0)),
            pl.BlockSpec((1, N_ADA, d), lambda i, j: (i // nsb, 0, 0)),
            pl.BlockSpec((d, tn), lambda i, j: (0, j)),
            pl.BlockSpec((1, 1, tn), lambda i, j: (j, 0, 0)),
            pl.BlockSpec((tm, LANES), lambda i, j: (i % nsb, 0)),
            pl.BlockSpec((tm, LANES), lambda i, j: (i % nsb, 0)),
        ],
        out_specs=pl.BlockSpec((tm, tn), lambda i, j: (i, j)),
        out_shape=jax.ShapeDtypeStruct((t, n), BF16),
        scratch_shapes=[pltpu.VMEM((tm, d), BF16)],
        compiler_params=_cparams(("arbitrary", "arbitrary")),
        name="proj_a",
    )(x2, mod, w, gains, cos, sin)


def _proj_b_kernel(x_ref, mod_ref, w_ref, gain_ref, cos_ref, sin_ref, ci_ref, sa_ref, sb_ref,
                   o_ref, wi_ref, h_ref, *, n_q, n_qi, tn, wi_scale):
    j = pl.program_id(1)

    @pl.when(j == 0)
    def _():
        h = _modulate(x_ref[...], mod_ref[0, 3:4, :], mod_ref[0, 4:5, :])
        h_ref[...] = h.astype(BF16)

    p = _dot(h_ref[...], w_ref[...])

    @pl.when(j < n_qi)
    def _():
        ci = ci_ref[...]
        sa = sa_ref[...]
        sb = sb_ref[...]
        for c in range(tn // LANES):
            sl = slice(c * LANES, (c + 1) * LANES)
            o_ref[:, sl] = _rope_partial(p[:, sl], ci, sa, sb).astype(BF16)

    @pl.when(jnp.logical_and(j >= n_qi, j < n_q + n_qi))
    def _():
        cos = cos_ref[...]
        sin = sin_ref[...]
        for c in range(tn // LANES):
            sl = slice(c * LANES, (c + 1) * LANES)
            r = _rms(p[:, sl]) * gain_ref[0, 0:1, sl]
            o_ref[:, sl] = _rope_full(r, cos, sin).astype(BF16)

    @pl.when(j == n_q + n_qi)
    def _():
        k = _rms(p[:, 0:LANES]) * gain_ref[0, 0:1, 0:LANES]
        o_ref[:, 0:LANES] = _rope_full(k, cos_ref[...], sin_ref[...]).astype(BF16)
        ki = p[:, LANES:2 * LANES]
        mu = jnp.mean(ki, axis=-1, keepdims=True)
        kc = ki - mu
        var = jnp.mean(kc * kc, axis=-1, keepdims=True)
        kn = kc * lax.rsqrt(var + NORM_EPS) * gain_ref[0, 0:1, LANES:2 * LANES] \
            + gain_ref[0, 1:2, LANES:2 * LANES]
        o_ref[:, LANES:2 * LANES] = _rope_partial(kn, ci_ref[...], sa_ref[...], sb_ref[...]).astype(BF16)
        o_ref[:, 2 * LANES:3 * LANES] = p[:, 2 * LANES:3 * LANES].astype(BF16)
        wi = p[:, 3 * LANES:4 * LANES] * wi_scale
        o_ref[:, 3 * LANES:4 * LANES] = wi.astype(BF16)
        wi_ref[...] = wi


def _proj_b(x2, mod, w, gains, cos, sin, ci, sa, sb, seq):
    t, d = x2.shape
    n = w.shape[1]
    tm = min(1024, seq)
    tn = 512
    nsb = seq // tm
    n_q = d // tn
    n_qi = IDX_HEADS * HEAD_DIM // tn
    assert n == (n_q + n_qi + 1) * tn
    wi_scale = (IDX_HEADS ** -0.5) * (HEAD_DIM ** -0.5)
    tab = pl.BlockSpec((tm, LANES), lambda i, j: (i % nsb, 0))
    return pl.pallas_call(
        functools.partial(_proj_b_kernel, n_q=n_q, n_qi=n_qi, tn=tn, wi_scale=wi_scale),
        grid=(t // tm, n // tn),
        in_specs=[
            pl.BlockSpec((tm, d), lambda i, j: (i, 0)),
            pl.BlockSpec((1, N_ADA, d), lambda i, j: (i // nsb, 0, 0)),
            pl.BlockSpec((d, tn), lambda i, j: (0, j)),
            pl.BlockSpec((1, 2, tn), lambda i, j: (j, 0, 0)),
            tab, tab, tab, tab, tab,
        ],
        out_specs=[
            pl.BlockSpec((tm, tn), lambda i, j: (i, j)),
            pl.BlockSpec((tm, LANES), lambda i, j: (i, 0)),
        ],
        out_shape=[
            jax.ShapeDtypeStruct((t, n), BF16),
            jax.ShapeDtypeStruct((t, LANES), F32),
        ],
        scratch_shapes=[pltpu.VMEM((tm, d), BF16)],
        compiler_params=_cparams(("arbitrary", "arbitrary")),
        name="proj_b",
    )(x2, mod, w, gains, cos, sin, ci, sa, sb)


def _out_kernel(o_ref, w_ref, x_ref, mod_ref, y_ref):
    y = _dot(o_ref[...], w_ref[...])
    y_ref[...] = x_ref[...] + mod_ref[0, 5:6, :] * y


def _out_proj(o2, w, x2, mod, seq):
    t, d = x2.shape
    k = o2.shape[1]
    tm = min(512, seq)
    nsb = seq // tm
    return pl.pallas_call(
        _out_kernel,
        grid=(t // tm,),
        in_specs=[
            pl.BlockSpec((tm, k), lambda i: (i, 0)),
            pl.BlockSpec((k, d), lambda i: (0, 0)),
            pl.BlockSpec((tm, d), lambda i: (i, 0)),
            pl.BlockSpec((1, N_ADA, d), lambda i: (i // nsb, 0, 0)),
        ],
        out_specs=pl.BlockSpec((tm, d), lambda i: (i, 0)),
        out_shape=jax.ShapeDtypeStruct((t, d), F32),
        compiler_params=_cparams(("arbitrary",)),
        name="out_proj",
    )(o2, w, x2, mod)


def _softmax_step(s, m_ref, l_ref, acc_ref, v, idx, rows=slice(None)):
    tk = s.shape[1]
    dv = v.shape[1]
    m_prev = m_ref[idx, rows, :]
    m_new = jnp.maximum(m_prev, jnp.max(s, axis=1, keepdims=True))
    alpha = jnp.exp2(m_prev - m_new)
    p = jnp.exp2(s - _tile_lanes(m_new, tk // LANES))
    l_ref[idx, rows, :] = alpha * l_ref[idx, rows, :] + jnp.sum(p, axis=1, keepdims=True)
    acc_ref[idx, rows, :] = _tile_lanes(alpha, dv // LANES) * acc_ref[idx, rows, :] + _dot(p.astype(BF16), v)
    m_ref[idx, rows, :] = m_new


def _attn_a_kernel(q0_ref, q1_ref, k0_ref, k1_ref, v_ref, lam_ref, gain_ref, o_ref,
                   m_ref, l_ref, acc_ref, *, tq, lam_init):
    qi = pl.program_id(2)
    m_ref[...] = jnp.full_like(m_ref, NEG)
    l_ref[...] = jnp.zeros_like(l_ref)
    acc_ref[...] = jnp.zeros_like(acc_ref)
    tk = tq
    q0 = q0_ref[0]
    q1 = q1_ref[0]

    def body(c, carry):
        off = pl.multiple_of(c * tk, tk)
        v = v_ref[0, pl.ds(off, tk), :]
        s0 = _dot_nt(q0, k0_ref[0, pl.ds(off, tk), :])
        s1 = _dot_nt(q1, k1_ref[0, pl.ds(off, tk), :])
        _softmax_step(s0, m_ref, l_ref, acc_ref, v, 0)
        _softmax_step(s1, m_ref, l_ref, acc_ref, v, 1)
        return carry

    lax.fori_loop(0, qi, body, 0)

    off = pl.multiple_of(qi * tk, tk)
    band = tq // 2
    for r in range(2):
        rows = slice(r * band, (r + 1) * band)
        width = (r + 1) * band
        v = v_ref[0, pl.ds(off, width), :]
        row = r * band + lax.broadcasted_iota(jnp.int32, (band, width), 0)
        keep = lax.broadcasted_iota(jnp.int32, (band, width), 1) <= row
        s0 = _dot_nt(q0_ref[0, rows, :], k0_ref[0, pl.ds(off, width), :])
        s1 = _dot_nt(q1_ref[0, rows, :], k1_ref[0, pl.ds(off, width), :])
        _softmax_step(jnp.where(keep, s0, NEG), m_ref, l_ref, acc_ref, v, 0, rows)
        _softmax_step(jnp.where(keep, s1, NEG), m_ref, l_ref, acc_ref, v, 1, rows)

    dv = acc_ref.shape[2]
    lam_q = lam_ref[...]
    s1 = jnp.sum(lam_q[0:1, :] * lam_q[1:2, :], axis=1, keepdims=True)
    s2 = jnp.sum(lam_q[2:3, :] * lam_q[3:4, :], axis=1, keepdims=True)
    lam = jnp.exp(s1) - jnp.exp(s2) + lam_init
    o0 = acc_ref[0] * _tile_lanes(1.0 / l_ref[0], dv // LANES)
    o1 = acc_ref[1] * _tile_lanes(1.0 / l_ref[1], dv // LANES)
    o = o0 - lam * o1
    o_ref[0] = (_rms(o) * gain_ref[...]).astype(BF16)


def _attn_a(p3, lam_vecs, gain_eff, n_heads, lam_init):
    b, s, n = p3.shape
    d = n // 3
    tq = min(1024, s)
    hd = HEAD_DIM
    kcol = d // hd
    vcol = 2 * d // (2 * hd)
    return pl.pallas_call(
        functools.partial(_attn_a_kernel, tq=tq, lam_init=lam_init),
        grid=(b, n_heads, s // tq),
        in_specs=[
            pl.BlockSpec((1, tq, hd), lambda bi, h, i: (bi, i, 2 * h)),
            pl.BlockSpec((1, tq, hd), lambda bi, h, i: (bi, i, 2 * h + 1)),
            pl.BlockSpec((1, s, hd), lambda bi, h, i: (bi, 0, kcol + 2 * h)),
            pl.BlockSpec((1, s, hd), lambda bi, h, i: (bi, 0, kcol + 2 * h + 1)),
            pl.BlockSpec((1, s, 2 * hd), lambda bi, h, i: (bi, 0, vcol + h)),
            pl.BlockSpec((4, hd), lambda bi, h, i: (0, 0)),
            pl.BlockSpec((1, 2 * hd), lambda bi, h, i: (0, 0)),
        ],
        out_specs=pl.BlockSpec((1, tq, 2 * hd), lambda bi, h, i: (bi, i, h)),
        out_shape=jax.ShapeDtypeStruct((b, s, d), BF16),
        scratch_shapes=[
            pltpu.VMEM((2, tq, LANES), F32),
            pltpu.VMEM((2, tq, LANES), F32),
            pltpu.VMEM((2, tq, 2 * hd), F32),
        ],
        compiler_params=_cparams(("arbitrary", "arbitrary", "arbitrary")),
        name="attn_a",
    )(p3, p3, p3, p3, p3, lam_vecs, gain_eff)


def _attn_b_kernel(q_ref, qi_ref, wi_ref, k_ref, ki_ref, vt_ref, o_ref,
                   keys_ref, m_ref, acc_ref, *, tq, tk, n_heads, topk, idx_bits):
    i = pl.program_id(1)
    n_keys = (i + 1) * tq
    n_full = n_keys // tk
    has_tail = n_keys % tk != 0
    nkc = (n_keys + tk - 1) // tk
    kf = float(topk)

    def key_pos(off, rows):
        return off + lax.broadcasted_iota(jnp.int32, (rows, tq), 0)

    def score_block(off, width):
        kic = ki_ref[0, pl.ds(off, width), :]
        sc = jnp.zeros((width, tq), F32)
        for h in range(IDX_HEADS):
            s = _dot_nt(kic, qi_ref[0, :, h * HEAD_DIM:(h + 1) * HEAD_DIM])
            sc = sc + wi_ref[0, h:h + 1, :] * jnp.maximum(s, 0.0)
        bits = pltpu.bitcast(sc, jnp.int32)
        key = bits ^ ((bits >> 31) & jnp.int32(0x7FFFFFFF))
        qpos = i * tq + lax.broadcasted_iota(jnp.int32, (width, tq), 1)
        keys_ref[pl.ds(off, width), :] = jnp.where(key_pos(off, width) <= qpos, key, INT_MIN)

    def score_pair(j, carry):
        score_block(pl.multiple_of(2 * j * tk, tk), tk)
        score_block(pl.multiple_of((2 * j + 1) * tk, tk), tk)
        return carry

    lax.fori_loop(0, n_full // 2, score_pair, 0)

    @pl.when(n_full % 2 == 1)
    def _():
        score_block(pl.multiple_of((n_full - 1) * tk, tk), tk)

    if tk > tq:
        @pl.when(has_tail)
        def _():
            off = pl.multiple_of(n_full * tk, tk)
            score_block(off, tq)
            keys_ref[pl.ds(off + tq, tk - tq), :] = jnp.full((tk - tq, tq), INT_MIN, jnp.int32)

    def col_total(part):
        return jnp.sum(part.astype(F32), axis=0, keepdims=True)

    def fold_chunk(off, part, hit_fn):
        for r in range(tk // ACC_CHAINS):
            ro = off + r * ACC_CHAINS
            part = part + jnp.where(hit_fn(keys_ref[pl.ds(ro, ACC_CHAINS), :], ro), 1, 0)
        return part

    def count_ge(thr):
        def cbody(c, part):
            off = pl.multiple_of(c * tk, tk)
            return fold_chunk(off, part, lambda kk, ro: kk >= thr)
        return col_total(lax.fori_loop(0, nkc, cbody, jnp.zeros((ACC_CHAINS, tq), jnp.int32)))

    def search(it, thr):
        cand = thr + jnp.left_shift(jnp.int32(1), 31 - it)
        return jnp.where(count_ge(cand) >= kf, cand, thr)

    thr = lax.fori_loop(0, 32, search, jnp.full((1, tq), INT_MIN, jnp.int32))

    tie = jnp.where(jnp.logical_and(count_ge(thr) > kf, thr > INT_MIN), 1.0, 0.0)
    any_tie = jnp.max(tie)

    @pl.when(any_tie > 0.0)
    def _():
        need = kf - count_ge(thr + 1)

        def count_eq_before(jcut):
            def cbody(c, part):
                off = pl.multiple_of(c * tk, tk)
                return fold_chunk(off, part, lambda kk, ro: jnp.logical_and(
                    kk == thr, key_pos(ro, ACC_CHAINS) < jcut))
            return col_total(lax.fori_loop(0, nkc, cbody, jnp.zeros((ACC_CHAINS, tq), jnp.int32)))

        def jsearch(it, jcut):
            cand = jcut + jnp.left_shift(jnp.int32(1), idx_bits - 1 - it)
            return jnp.where(count_eq_before(cand) <= need, cand, jcut)

        jcut = lax.fori_loop(0, idx_bits, jsearch, jnp.zeros((1, tq), jnp.int32))

        def demote(c, carry):
            off = pl.multiple_of(c * tk, tk)
            kk = keys_ref[pl.ds(off, tk), :]
            drop = jnp.logical_and(jnp.logical_and(kk == thr, key_pos(off, tk) >= jcut), tie > 0.0)
            keys_ref[pl.ds(off, tk), :] = jnp.where(drop, kk - 1, kk)
            return carry

        lax.fori_loop(0, nkc, demote, 0)

    thr_sel = jnp.maximum(thr, INT_MIN + 1)
    m_ref[...] = jnp.full_like(m_ref, NEG)
    acc_ref[...] = jnp.zeros_like(acc_ref)

    def attend(c, width=tk):
        off = pl.multiple_of(c * tk, tk)
        kc = k_ref[0, pl.ds(off, width), :]
        vt = vt_ref[0, :, pl.ds(off, width)]
        bias = jnp.where(keys_ref[pl.ds(off, width), :] >= thr_sel, 0.0, NEG)
        for h in range(n_heads):
            s = _dot_nt(kc, q_ref[0, :, h * HEAD_DIM:(h + 1) * HEAD_DIM]) + bias
            m_prev = m_ref[h]
            m_new = jnp.maximum(m_prev, jnp.max(s, axis=0, keepdims=True))
            alpha = jnp.exp2(m_prev - m_new)
            p = jnp.exp2(s - m_new)
            acc_ref[h] = alpha * acc_ref[h] + _dot(vt, p.astype(BF16))
            m_ref[h] = m_new

    def attend_quad(j, carry):
        for u in range(4):
            attend(4 * j + u)
        return carry

    lax.fori_loop(0, n_full // 4, attend_quad, 0)

    @pl.when(n_full % 4 >= 2)
    def _():
        attend((n_full // 4) * 4)
        attend((n_full // 4) * 4 + 1)

    @pl.when(n_full % 2 == 1)
    def _():
        attend(n_full - 1)

    if tk > tq:
        @pl.when(has_tail)
        def _():
            attend(n_full, tq)

    for h in range(n_heads):
        a = acc_ref[h]
        o = a[0:HEAD_DIM, :] * (1.0 / a[HEAD_DIM:HEAD_DIM + 1, :])
        o_ref[0, :, h * HEAD_DIM:(h + 1) * HEAD_DIM] = o.T.astype(BF16)


def _attn_b(p3, wi_t, v1t, n_heads, topk):
    b, s, n = p3.shape
    tq = min(256, s)
    tk = min(512, s)
    d = n_heads * HEAD_DIM
    qi_w = IDX_HEADS * HEAD_DIM
    assert qi_w % d == 0
    kcol = (d + qi_w) // HEAD_DIM
    vrows = v1t.shape[1]
    assert vrows == 2 * HEAD_DIM
    assert tk in (tq, 2 * tq)
    return pl.pallas_call(
        functools.partial(_attn_b_kernel, tq=tq, tk=tk, n_heads=n_heads, topk=topk,
                          idx_bits=int(s).bit_length()),
        grid=(b, s // tq),
        in_specs=[
            pl.BlockSpec((1, tq, d), lambda bi, i: (bi, i, qi_w // d)),
            pl.BlockSpec((1, tq, qi_w), lambda bi, i: (bi, i, 0)),
            pl.BlockSpec((1, LANES, tq), lambda bi, i: (bi, 0, i)),
            pl.BlockSpec((1, s, HEAD_DIM), lambda bi, i: (bi, 0, kcol)),
            pl.BlockSpec((1, s, HEAD_DIM), lambda bi, i: (bi, 0, kcol + 1)),
            pl.BlockSpec((1, vrows, s), lambda bi, i: (bi, 0, 0)),
        ],
        out_specs=pl.BlockSpec((1, tq, d), lambda bi, i: (bi, i, 0)),
        out_shape=jax.ShapeDtypeStruct((b, s, d), BF16),
        scratch_shapes=[
            pltpu.VMEM((s, tq), jnp.int32),
            pltpu.VMEM((n_heads, 1, tq), F32),
            pltpu.VMEM((n_heads, vrows, tq), F32),
        ],
        compiler_params=_cparams(("arbitrary", "arbitrary")),
        name="attn_b",
    )(p3, p3, wi_t, p3, p3, v1t)


def _rope_tables(seq):
    pos = jnp.arange(seq, dtype=F32)[:, None]

    def cs(dim):
        inv = 1.0 / (ROPE_THETA ** (jnp.arange(0, dim, 2, dtype=F32) / dim))
        ang = pos * inv[None, :]
        return jnp.cos(ang), jnp.sin(ang)

    c, s = cs(HEAD_DIM)
    cos = jnp.concatenate([c, c], axis=1)
    sin = jnp.concatenate([-s, s], axis=1)
    c, s = cs(IDX_ROPE_DIM)
    one = jnp.ones((seq, LANES - IDX_ROPE_DIM), F32)
    z32 = jnp.zeros_like(s)
    z64 = jnp.zeros_like(one)
    ci = jnp.concatenate([c, c, one], axis=1)
    sa = jnp.concatenate([-s, z32, z64], axis=1)
    sb = jnp.concatenate([z32, s, z64], axis=1)
    return cos, sin, ci, sa, sb


def kernel(x, c, ada_w, ada_b, ffn_w_gate, ffn_w_up, ffn_w_down, a_w_in, a_w_out, a_q_gain, a_k_gain,
           a_lambda_q1, a_lambda_k1, a_lambda_q2, a_lambda_k2, a_subln_gain, b_w_in, b_w_out, b_q_gain,
           b_k_gain, b_kidx_gain, b_kidx_bias):
    b, s, d = x.shape
    depth = ada_w.shape[0]
    hd = HEAD_DIM
    a_heads = d // (2 * hd)
    b_heads = d // hd
    qscale = (hd ** -0.5) * math.log2(math.e)
    topk = min(TOPK_MAX, s // 4)
    tn = 512

    cos, sin, ci, sa, sb = _rope_tables(s)
    mod = _ada_mod(c, ada_w, ada_b)
    wg = ffn_w_gate.astype(BF16)
    wu = ffn_w_up.astype(BF16)
    wd = ffn_w_down.astype(BF16)

    x2 = x.reshape(b * s, d)
    for i in range(depth):
        j = i // 2
        x2 = _ffn(x2, mod[i], wg, wu, wd, i, 0, s)
        if i % 2 == 0:
            lam_init = 0.8 - 0.6 * math.exp(-0.3 * i)
            reps = tn // hd
            gains = jnp.concatenate([
                jnp.tile(jnp.tile(a_q_gain[j] * qscale, reps)[None], (d // tn, 1)),
                jnp.tile(jnp.tile(a_k_gain[j], reps)[None], (d // tn, 1)),
                jnp.ones((d // tn, tn), F32),
            ], axis=0)[:, None, :]
            p = _proj_a(x2, mod[i], a_w_in[j].astype(BF16), gains, cos, sin, s)
            lam_vecs = jnp.stack([a_lambda_q1[j], a_lambda_k1[j], a_lambda_q2[j], a_lambda_k2[j]])
            gain_eff = (a_subln_gain[j] * (1.0 - lam_init))[None, :]
            o = _attn_a(p.reshape(b, s, 3 * d), lam_vecs, gain_eff, a_heads, lam_init)
            x2 = _out_proj(o.reshape(b * s, d), a_w_out[j].astype(BF16), x2, mod[i], s)
        else:
            w = b_w_in[j]
            q_w = b_heads * hd
            qi_w = IDX_HEADS * hd
            o_k, o_v, o_qi, o_ki, o_wi = q_w, q_w + hd, q_w + 2 * hd, q_w + 2 * hd + qi_w, q_w + 3 * hd + qi_w
            w_re = jnp.concatenate([
                w[:, o_qi:o_qi + qi_w], w[:, :q_w], w[:, o_k:o_k + hd], w[:, o_ki:o_ki + hd],
                w[:, o_v:o_v + hd], w[:, o_wi:o_wi + IDX_HEADS],
                jnp.zeros((d, hd - IDX_HEADS), F32),
            ], axis=1).astype(BF16)
            n_tiles = w_re.shape[1] // tn
            reps = tn // hd
            g0 = jnp.zeros((n_tiles, 2, tn), F32)
            g0 = g0.at[qi_w // tn:(qi_w + q_w) // tn, 0, :].set(jnp.tile(b_q_gain[j] * qscale, reps)[None])
            g0 = g0.at[n_tiles - 1, 0, 0:hd].set(b_k_gain[j])
            g0 = g0.at[n_tiles - 1, 0, hd:2 * hd].set(b_kidx_gain[j])
            g0 = g0.at[n_tiles - 1, 1, hd:2 * hd].set(b_kidx_bias[j])
            p, wi = _proj_b(x2, mod[i], w_re, g0, cos, sin, ci, sa, sb, s)
            p3 = p.reshape(b, s, w_re.shape[1])
            v_t = jnp.swapaxes(p3[:, :, qi_w + q_w + 2 * hd:qi_w + q_w + 3 * hd], 1, 2)
            v1t = jnp.concatenate([v_t, jnp.ones_like(v_t)], axis=1)
            wi_t = jnp.swapaxes(wi.reshape(b, s, LANES), 1, 2)
            o = _attn_b(p3, wi_t, v1t, b_heads, topk)
            x2 = _out_proj(o.reshape(b * s, d), b_w_out[j].astype(BF16), x2, mod[i], s)
        x2 = _ffn(x2, mod[i], wg, wu, wd, i, 1, s)
    return x2.reshape(b, s, d)
```

```python
import functools
import math

import jax
import jax.numpy as jnp
import numpy as np
from jax import lax
from jax.experimental import pallas as pl
from jax.experimental.pallas import tpu as pltpu

ROPE_THETA = 10000.0
NORM_EPS = 1e-6
N_ADA = 9
HEAD_DIM = 128
IDX_HEADS = 16
IDX_ROPE_DIM = 64
TOPK_MAX = 256
LANES = 128
ACC_CHAINS = 64
V7X_VMEM_LIMIT = 56 * 1024 * 1024
NEG = -1e30
INT_MIN = np.int32(-(2 ** 31))

BF16 = jnp.bfloat16
F32 = jnp.float32


def _cparams(sem):
    return pltpu.CompilerParams(dimension_semantics=sem, vmem_limit_bytes=V7X_VMEM_LIMIT)


def _dot(a, b):
    return jnp.dot(a, b, preferred_element_type=F32)


def _dot_nt(a, b):
    return lax.dot_general(a, b, (((1,), (1,)), ((), ())), preferred_element_type=F32)


def _tile_lanes(x, n):
    return x if n == 1 else jnp.concatenate([x] * n, axis=1)


def _modulate(x, shift, scale):
    ms = jnp.mean(x * x, axis=-1, keepdims=True)
    return x * lax.rsqrt(ms + NORM_EPS) * (1.0 + scale) + shift


def _rms(x):
    return x * lax.rsqrt(jnp.mean(x * x, axis=-1, keepdims=True) + NORM_EPS)


def _ada_kernel(c_ref, w_ref, b_ref, o_ref):
    c = c_ref[...]
    ca = (c * (1.0 / (1.0 + jnp.exp(-c)))).astype(BF16)
    o_ref[0] = _dot(ca, w_ref[0].astype(BF16)) + b_ref[0]


def _ada_mod(c, ada_w, ada_b):
    depth, d, n = ada_w.shape
    b = c.shape[0]
    rows = 8
    c8 = jnp.zeros((rows, d), F32).at[:b].set(c)
    tn = math.gcd(n, 1024)
    out = pl.pallas_call(
        _ada_kernel,
        grid=(depth, n // tn),
        in_specs=[
            pl.BlockSpec((rows, d), lambda i, j: (0, 0)),
            pl.BlockSpec((1, d, tn), lambda i, j: (i, 0, j)),
            pl.BlockSpec((1, 1, tn), lambda i, j: (i, 0, j)),
        ],
        out_specs=pl.BlockSpec((1, rows, tn), lambda i, j: (i, 0, j)),
        out_shape=jax.ShapeDtypeStruct((depth, rows, n), F32),
        compiler_params=_cparams(("arbitrary", "arbitrary")),
        name="ada_mod",
    )(c8, ada_w, ada_b.reshape(depth, 1, n))
    return out[:, :b].reshape(depth, b, N_ADA, d)


def _ffn_kernel(x_ref, mod_ref, wg_ref, wu_ref, wd_ref, o_ref, h_ref, *, row0, nf):
    f = pl.program_id(1)

    def down_of_tile():
        h = h_ref[...]
        a = _dot(h, wg_ref[...])
        u = _dot(h, wu_ref[...])
        act = (a * (1.0 / (1.0 + jnp.exp(-a))) * u).astype(BF16)
        return _dot(act, wd_ref[...])

    assert nf >= 2

    @pl.when(f == 0)
    def _():
        h = _modulate(x_ref[...], mod_ref[0, row0:row0 + 1, :], mod_ref[0, row0 + 1:row0 + 2, :])
        h_ref[...] = h.astype(BF16)
        o_ref[...] = down_of_tile()

    @pl.when(jnp.logical_and(f > 0, f < nf - 1))
    def _():
        o_ref[...] += down_of_tile()

    @pl.when(f == nf - 1)
    def _():
        g = mod_ref[0, row0 + 2:row0 + 3, :]
        o_ref[...] = x_ref[...] + 0.5 * g * (o_ref[...] + down_of_tile())


def _ffn(x2, mod, wg, wu, wd, layer, which, seq):
    t, d = x2.shape
    f_dim = wg.shape[3]
    tm = min(1024, seq)
    tf = 512 if f_dim % 512 == 0 else f_dim
    nsb = seq // tm
    nf = f_dim // tf
    row0 = 6 * which
    return pl.pallas_call(
        functools.partial(_ffn_kernel, row0=row0, nf=nf),
        grid=(t // tm, nf),
        in_specs=[
            pl.BlockSpec((tm, d), lambda i, f: (i, 0)),
            pl.BlockSpec((1, N_ADA, d), lambda i, f: (i // nsb, 0, 0)),
            pl.BlockSpec((None, None, d, tf), lambda i, f: (layer, which, 0, f)),
            pl.BlockSpec((None, None, d, tf), lambda i, f: (layer, which, 0, f)),
            pl.BlockSpec((None, None, tf, d), lambda i, f: (layer, which, f, 0)),
        ],
        out_specs=pl.BlockSpec((tm, d), lambda i, f: (i, 0)),
        out_shape=jax.ShapeDtypeStruct((t, d), F32),
        scratch_shapes=[pltpu.VMEM((tm, d), BF16)],
        compiler_params=_cparams(("arbitrary", "arbitrary")),
        name="ffn",
    )(x2, mod, wg, wu, wd)


def _rope_full(r, cos, sin):
    return r * cos + pltpu.roll(r, HEAD_DIM // 2, 1) * sin


def _rope_partial(r, ci, sa, sb):
    q = IDX_ROPE_DIM // 2
    return r * ci + pltpu.roll(r, LANES - q, 1) * sa + pltpu.roll(r, q, 1) * sb


def _proj_a_kernel(x_ref, mod_ref, w_ref, gain_ref, cos_ref, sin_ref, o_ref, h_ref, *, n_qk, tn):
    j = pl.program_id(1)

    @pl.when(j == 0)
    def _():
        h = _modulate(x_ref[...], mod_ref[0, 3:4, :], mod_ref[0, 4:5, :])
        h_ref[...] = h.astype(BF16)

    p = _dot(h_ref[...], w_ref[...])

    @pl.when(j < n_qk)
    def _():
        cos = cos_ref[...]
        sin = sin_ref[...]
        for c in range(tn // LANES):
            sl = slice(c * LANES, (c + 1) * LANES)
            r = _rms(p[:, sl]) * gain_ref[0, :, sl]
            o_ref[:, sl] = _rope_full(r, cos, sin).astype(BF16)

    @pl.when(j >= n_qk)
    def _():
        o_ref[...] = p.astype(BF16)


def _proj_a(x2, mod, w, gains, cos, sin, seq):
    t, d = x2.shape
    n = w.shape[1]
    tm = min(1024, seq)
    tn = 512
    nsb = seq // tm
    n_qk = 2 * d // tn
    return pl.pallas_call(
        functools.partial(_proj_a_kernel, n_qk=n_qk, tn=tn),
        grid=(t // tm, n // tn),
        in_specs=[
            pl.BlockSpec((tm, d), lambda i, j: (i, 0)),
            pl.BlockSpec((1, N_ADA, d), lambda i, j: (i // nsb, 0, 0)),
            pl.BlockSpec((d, tn), lambda i, j: (0, j)),
            pl.BlockSpec((1, 1, tn), lambda i, j: (j, 0, 0)),
            pl.BlockSpec((tm, LANES), lambda i, j: (i % nsb, 0)),
            pl.BlockSpec((tm, LANES), lambda i, j: (i % nsb, 0)),
        ],
        out_specs=pl.BlockSpec((tm, tn), lambda i, j: (i, j)),
        out_shape=jax.ShapeDtypeStruct((t, n), BF16),
        scratch_shapes=[pltpu.VMEM((tm, d), BF16)],
        compiler_params=_cparams(("arbitrary", "arbitrary")),
        name="proj_a",
    )(x2, mod, w, gains, cos, sin)


def _proj_b_kernel(x_ref, mod_ref, w_ref, gain_ref, cos_ref, sin_ref, ci_ref, sa_ref, sb_ref,
                   o_ref, wi_ref, h_ref, *, n_q, n_qi, tn, wi_scale):
    j = pl.program_id(1)

    @pl.when(j == 0)
    def _():
        h = _modulate(x_ref[...], mod_ref[0, 3:4, :], mod_ref[0, 4:5, :])
        h_ref[...] = h.astype(BF16)

    p = _dot(h_ref[...], w_ref[...])

    @pl.when(j < n_qi)
    def _():
        ci = ci_ref[...]
        sa = sa_ref[...]
        sb = sb_ref[...]
        for c in range(tn // LANES):
            sl = slice(c * LANES, (c + 1) * LANES)
            o_ref[:, sl] = _rope_partial(p[:, sl], ci, sa, sb).astype(BF16)

    @pl.when(jnp.logical_and(j >= n_qi, j < n_q + n_qi))
    def _():
        cos = cos_ref[...]
        sin = sin_ref[...]
        for c in range(tn // LANES):
            sl = slice(c * LANES, (c + 1) * LANES)
            r = _rms(p[:, sl]) * gain_ref[0, 0:1, sl]
            o_ref[:, sl] = _rope_full(r, cos, sin).astype(BF16)

    @pl.when(j == n_q + n_qi)
    def _():
        k = _rms(p[:, 0:LANES]) * gain_ref[0, 0:1, 0:LANES]
        o_ref[:, 0:LANES] = _rope_full(k, cos_ref[...], sin_ref[...]).astype(BF16)
        ki = p[:, LANES:2 * LANES]
        mu = jnp.mean(ki, axis=-1, keepdims=True)
        kc = ki - mu
        var = jnp.mean(kc * kc, axis=-1, keepdims=True)
        kn = kc * lax.rsqrt(var + NORM_EPS) * gain_ref[0, 0:1, LANES:2 * LANES] \
            + gain_ref[0, 1:2, LANES:2 * LANES]
        o_ref[:, LANES:2 * LANES] = _rope_partial(kn, ci_ref[...], sa_ref[...], sb_ref[...]).astype(BF16)
        o_ref[:, 2 * LANES:3 * LANES] = p[:, 2 * LANES:3 * LANES].astype(BF16)
        wi = p[:, 3 * LANES:4 * LANES] * wi_scale
        o_ref[:, 3 * LANES:4 * LANES] = wi.astype(BF16)
        wi_ref[...] = wi


def _proj_b(x2, mod, w, gains, cos, sin, ci, sa, sb, seq):
    t, d = x2.shape
    n = w.shape[1]
    tm = min(1024, seq)
    tn = 512
    nsb = seq // tm
    n_q = d // tn
    n_qi = IDX_HEADS * HEAD_DIM // tn
    assert n == (n_q + n_qi + 1) * tn
    wi_scale = (IDX_HEADS ** -0.5) * (HEAD_DIM ** -0.5)
    tab = pl.BlockSpec((tm, LANES), lambda i, j: (i % nsb, 0))
    return pl.pallas_call(
        functools.partial(_proj_b_kernel, n_q=n_q, n_qi=n_qi, tn=tn, wi_scale=wi_scale),
        grid=(t // tm, n // tn),
        in_specs=[
            pl.BlockSpec((tm, d), lambda i, j: (i, 0)),
            pl.BlockSpec((1, N_ADA, d), lambda i, j: (i // nsb, 0, 0)),
            pl.BlockSpec((d, tn), lambda i, j: (0, j)),
            pl.BlockSpec((1, 2, tn), lambda i, j: (j, 0, 0)),
            tab, tab, tab, tab, tab,
        ],
        out_specs=[
            pl.BlockSpec((tm, tn), lambda i, j: (i, j)),
            pl.BlockSpec((tm, LANES), lambda i, j: (i, 0)),
        ],
        out_shape=[
            jax.ShapeDtypeStruct((t, n), BF16),
            jax.ShapeDtypeStruct((t, LANES), F32),
        ],
        scratch_shapes=[pltpu.VMEM((tm, d), BF16)],
        compiler_params=_cparams(("arbitrary", "arbitrary")),
        name="proj_b",
    )(x2, mod, w, gains, cos, sin, ci, sa, sb)


def _out_kernel(o_ref, w_ref, x_ref, mod_ref, y_ref):
    y = _dot(o_ref[...], w_ref[...])
    y_ref[...] = x_ref[...] + mod_ref[0, 5:6, :] * y


def _out_proj(o2, w, x2, mod, seq):
    t, d = x2.shape
    k = o2.shape[1]
    tm = min(512, seq)
    nsb = seq // tm
    return pl.pallas_call(
        _out_kernel,
        grid=(t // tm,),
        in_specs=[
            pl.BlockSpec((tm, k), lambda i: (i, 0)),
            pl.BlockSpec((k, d), lambda i: (0, 0)),
            pl.BlockSpec((tm, d), lambda i: (i, 0)),
            pl.BlockSpec((1, N_ADA, d), lambda i: (i // nsb, 0, 0)),
        ],
        out_specs=pl.BlockSpec((tm, d), lambda i: (i, 0)),
        out_shape=jax.ShapeDtypeStruct((t, d), F32),
        compiler_params=_cparams(("arbitrary",)),
        name="out_proj",
    )(o2, w, x2, mod)


def _softmax_step(s, m_ref, l_ref, acc_ref, v, idx, rows=slice(None)):
    tk = s.shape[1]
    dv = v.shape[1]
    m_prev = m_ref[idx, rows, :]
    m_new = jnp.maximum(m_prev, jnp.max(s, axis=1, keepdims=True))
    alpha = jnp.exp2(m_prev - m_new)
    p = jnp.exp2(s - _tile_lanes(m_new, tk // LANES))
    l_ref[idx, rows, :] = alpha * l_ref[idx, rows, :] + jnp.sum(p, axis=1, keepdims=True)
    acc_ref[idx, rows, :] = _tile_lanes(alpha, dv // LANES) * acc_ref[idx, rows, :] + _dot(p.astype(BF16), v)
    m_ref[idx, rows, :] = m_new


def _attn_a_kernel(q0_ref, q1_ref, k0_ref, k1_ref, v_ref, lam_ref, gain_ref, o_ref,
                   m_ref, l_ref, acc_ref, *, tq, lam_init):
    qi = pl.program_id(2)
    m_ref[...] = jnp.full_like(m_ref, NEG)
    l_ref[...] = jnp.zeros_like(l_ref)
    acc_ref[...] = jnp.zeros_like(acc_ref)
    tk = tq
    q0 = q0_ref[0]
    q1 = q1_ref[0]

    def body(c, carry):
        off = pl.multiple_of(c * tk, tk)
        v = v_ref[0, pl.ds(off, tk), :]
        s0 = _dot_nt(q0, k0_ref[0, pl.ds(off, tk), :])
        s1 = _dot_nt(q1, k1_ref[0, pl.ds(off, tk), :])
        _softmax_step(s0, m_ref, l_ref, acc_ref, v, 0)
        _softmax_step(s1, m_ref, l_ref, acc_ref, v, 1)
        return carry

    lax.fori_loop(0, qi, body, 0)

    off = pl.multiple_of(qi * tk, tk)
    band = tq // 2
    for r in range(2):
        rows = slice(r * band, (r + 1) * band)
        width = (r + 1) * band
        v = v_ref[0, pl.ds(off, width), :]
        row = r * band + lax.broadcasted_iota(jnp.int32, (band, width), 0)
        keep = lax.broadcasted_iota(jnp.int32, (band, width), 1) <= row
        s0 = _dot_nt(q0_ref[0, rows, :], k0_ref[0, pl.ds(off, width), :])
        s1 = _dot_nt(q1_ref[0, rows, :], k1_ref[0, pl.ds(off, width), :])
        _softmax_step(jnp.where(keep, s0, NEG), m_ref, l_ref, acc_ref, v, 0, rows)
        _softmax_step(jnp.where(keep, s1, NEG), m_ref, l_ref, acc_ref, v, 1, rows)

    dv = acc_ref.shape[2]
    lam_q = lam_ref[...]
    s1 = jnp.sum(lam_q[0:1, :] * lam_q[1:2, :], axis=1, keepdims=True)
    s2 = jnp.sum(lam_q[2:3, :] * lam_q[3:4, :], axis=1, keepdims=True)
    lam = jnp.exp(s1) - jnp.exp(s2) + lam_init
    o0 = acc_ref[0] * _tile_lanes(1.0 / l_ref[0], dv // LANES)
    o1 = acc_ref[1] * _tile_lanes(1.0 / l_ref[1], dv // LANES)
    o = o0 - lam * o1
    o_ref[0] = (_rms(o) * gain_ref[...]).astype(BF16)


def _attn_a(p3, lam_vecs, gain_eff, n_heads, lam_init):
    b, s, n = p3.shape
    d = n // 3
    tq = min(1024, s)
    hd = HEAD_DIM
    kcol = d // hd
    vcol = 2 * d // (2 * hd)
    return pl.pallas_call(
        functools.partial(_attn_a_kernel, tq=tq, lam_init=lam_init),
        grid=(b, n_heads, s // tq),
        in_specs=[
            pl.BlockSpec((1, tq, hd), lambda bi, h, i: (bi, i, 2 * h)),
            pl.BlockSpec((1, tq, hd), lambda bi, h, i: (bi, i, 2 * h + 1)),
            pl.BlockSpec((1, s, hd), lambda bi, h, i: (bi, 0, kcol + 2 * h)),
            pl.BlockSpec((1, s, hd), lambda bi, h, i: (bi, 0, kcol + 2 * h + 1)),
            pl.BlockSpec((1, s, 2 * hd), lambda bi, h, i: (bi, 0, vcol + h)),
            pl.BlockSpec((4, hd), lambda bi, h, i: (0, 0)),
            pl.BlockSpec((1, 2 * hd), lambda bi, h, i: (0, 0)),
        ],
        out_specs=pl.BlockSpec((1, tq, 2 * hd), lambda bi, h, i: (bi, i, h)),
        out_shape=jax.ShapeDtypeStruct((b, s, d), BF16),
        scratch_shapes=[
            pltpu.VMEM((2, tq, LANES), F32),
            pltpu.VMEM((2, tq, LANES), F32),
            pltpu.VMEM((2, tq, 2 * hd), F32),
        ],
        compiler_params=_cparams(("arbitrary", "arbitrary", "arbitrary")),
        name="attn_a",
    )(p3, p3, p3, p3, p3, lam_vecs, gain_eff)


def _attn_b_kernel(q_ref, qi_ref, wi_ref, k_ref, ki_ref, vt_ref, o_ref,
                   keys_ref, m_ref, acc_ref, *, tq, tk, n_heads, topk, idx_bits):
    i = pl.program_id(1)
    n_keys = (i + 1) * tq
    n_full = n_keys // tk
    has_tail = n_keys % tk != 0
    nkc = (n_keys + tk - 1) // tk
    kf = float(topk)

    def key_pos(off, rows):
        return off + lax.broadcasted_iota(jnp.int32, (rows, tq), 0)

    def score_block(off, width):
        kic = ki_ref[0, pl.ds(off, width), :]
        sc = jnp.zeros((width, tq), F32)
        for h in range(IDX_HEADS):
            s = _dot_nt(kic, qi_ref[0, :, h * HEAD_DIM:(h + 1) * HEAD_DIM])
            sc = sc + wi_ref[0, h:h + 1, :] * jnp.maximum(s, 0.0)
        bits = pltpu.bitcast(sc, jnp.int32)
        key = bits ^ ((bits >> 31) & jnp.int32(0x7FFFFFFF))
        qpos = i * tq + lax.broadcasted_iota(jnp.int32, (width, tq), 1)
        keys_ref[pl.ds(off, width), :] = jnp.where(key_pos(off, width) <= qpos, key, INT_MIN)

    def score_pair(j, carry):
        score_block(pl.multiple_of(2 * j * tk, tk), tk)
        score_block(pl.multiple_of((2 * j + 1) * tk, tk), tk)
        return carry

    lax.fori_loop(0, n_full // 2, score_pair, 0)

    @pl.when(n_full % 2 == 1)
    def _():
        score_block(pl.multiple_of((n_full - 1) * tk, tk), tk)

    if tk > tq:
        @pl.when(has_tail)
        def _():
            off = pl.multiple_of(n_full * tk, tk)
            score_block(off, tq)
            keys_ref[pl.ds(off + tq, tk - tq), :] = jnp.full((tk - tq, tq), INT_MIN, jnp.int32)

    def col_total(part):
        return jnp.sum(part.astype(F32), axis=0, keepdims=True)

    def fold_chunk(off, part, hit_fn):
        for r in range(tk // ACC_CHAINS):
            ro = off + r * ACC_CHAINS
            part = part + jnp.where(hit_fn(keys_ref[pl.ds(ro, ACC_CHAINS), :], ro), 1, 0)
        return part

    def count_ge(thr):
        def cbody(c, part):
            off = pl.multiple_of(c * tk, tk)
            return fold_chunk(off, part, lambda kk, ro: kk >= thr)
        return col_total(lax.fori_loop(0, nkc, cbody, jnp.zeros((ACC_CHAINS, tq), jnp.int32)))

    def search(it, thr):
        cand = thr + jnp.left_shift(jnp.int32(1), 31 - it)
        return jnp.where(count_ge(cand) >= kf, cand, thr)

    thr = lax.fori_loop(0, 32, search, jnp.full((1, tq), INT_MIN, jnp.int32))

    tie = jnp.where(jnp.logical_and(count_ge(thr) > kf, thr > INT_MIN), 1.0, 0.0)
    any_tie = jnp.max(tie)

    @pl.when(any_tie > 0.0)
    def _():
        need = kf - count_ge(thr + 1)

        def count_eq_before(jcut):
            def cbody(c, part):
                off = pl.multiple_of(c * tk, tk)
                return fold_chunk(off, part, lambda kk, ro: jnp.logical_and(
                    kk == thr, key_pos(ro, ACC_CHAINS) < jcut))
            return col_total(lax.fori_loop(0, nkc, cbody, jnp.zeros((ACC_CHAINS, tq), jnp.int32)))

        def jsearch(it, jcut):
            cand = jcut + jnp.left_shift(jnp.int32(1), idx_bits - 1 - it)
            return jnp.where(count_eq_before(cand) <= need, cand, jcut)

        jcut = lax.fori_loop(0, idx_bits, jsearch, jnp.zeros((1, tq), jnp.int32))

        def demote(c, carry):
            off = pl.multiple_of(c * tk, tk)
            kk = keys_ref[pl.ds(off, tk), :]
            drop = jnp.logical_and(jnp.logical_and(kk == thr, key_pos(off, tk) >= jcut), tie > 0.0)
            keys_ref[pl.ds(off, tk), :] = jnp.where(drop, kk - 1, kk)
            return carry

        lax.fori_loop(0, nkc, demote, 0)

    thr_sel = jnp.maximum(thr, INT_MIN + 1)
    m_ref[...] = jnp.full_like(m_ref, NEG)
    acc_ref[...] = jnp.zeros_like(acc_ref)

    def attend(c, width=tk):
        off = pl.multiple_of(c * tk, tk)
        kc = k_ref[0, pl.ds(off, width), :]
        vt = vt_ref[0, :, pl.ds(off, width)]
        bias = jnp.where(keys_ref[pl.ds(off, width), :] >= thr_sel, 0.0, NEG)
        for h in range(n_heads):
            s = _dot_nt(kc, q_ref[0, :, h * HEAD_DIM:(h + 1) * HEAD_DIM]) + bias
            m_prev = m_ref[h]
            m_new = jnp.maximum(m_prev, jnp.max(s, axis=0, keepdims=True))
            alpha = jnp.exp2(m_prev - m_new)
            p = jnp.exp2(s - m_new)
            acc_ref[h] = alpha * acc_ref[h] + _dot(vt, p.astype(BF16))
            m_ref[h] = m_new

    def attend_quad(j, carry):
        for u in range(4):
            attend(4 * j + u)
        return carry

    lax.fori_loop(0, n_full // 4, attend_quad, 0)

    @pl.when(n_full % 4 >= 2)
    def _():
        attend((n_full // 4) * 4)
        attend((n_full // 4) * 4 + 1)

    @pl.when(n_full % 2 == 1)
    def _():
        attend(n_full - 1)

    if tk > tq:
        @pl.when(has_tail)
        def _():
            attend(n_full, tq)

    for h in range(n_heads):
        a = acc_ref[h]
        o = a[0:HEAD_DIM, :] * (1.0 / a[HEAD_DIM:HEAD_DIM + 1, :])
        o_ref[0, :, h * HEAD_DIM:(h + 1) * HEAD_DIM] = o.T.astype(BF16)


def _attn_b(p3, wi_t, v1t, n_heads, topk):
    b, s, n = p3.shape
    tq = min(256, s)
    tk = min(512, s)
    d = n_heads * HEAD_DIM
    qi_w = IDX_HEADS * HEAD_DIM
    assert qi_w % d == 0
    kcol = (d + qi_w) // HEAD_DIM
    vrows = v1t.shape[1]
    assert vrows == 2 * HEAD_DIM
    assert tk in (tq, 2 * tq)
    return pl.pallas_call(
        functools.partial(_attn_b_kernel, tq=tq, tk=tk, n_heads=n_heads, topk=topk,
                          idx_bits=int(s).bit_length()),
        grid=(b, s // tq),
        in_specs=[
            pl.BlockSpec((1, tq, d), lambda bi, i: (bi, i, qi_w // d)),
            pl.BlockSpec((1, tq, qi_w), lambda bi, i: (bi, i, 0)),
            pl.BlockSpec((1, LANES, tq), lambda bi, i: (bi, 0, i)),
            pl.BlockSpec((1, s, HEAD_DIM), lambda bi, i: (bi, 0, kcol)),
            pl.BlockSpec((1, s, HEAD_DIM), lambda bi, i: (bi, 0, kcol + 1)),
            pl.BlockSpec((1, vrows, s), lambda bi, i: (bi, 0, 0)),
        ],
        out_specs=pl.BlockSpec((1, tq, d), lambda bi, i: (bi, i, 0)),
        out_shape=jax.ShapeDtypeStruct((b, s, d), BF16),
        scratch_shapes=[
            pltpu.VMEM((s, tq), jnp.int32),
            pltpu.VMEM((n_heads, 1, tq), F32),
            pltpu.VMEM((n_heads, vrows, tq), F32),
        ],
        compiler_params=_cparams(("arbitrary", "arbitrary")),
        name="attn_b",
    )(p3, p3, wi_t, p3, p3, v1t)


def _rope_tables(seq):
    pos = jnp.arange(seq, dtype=F32)[:, None]

    def cs(dim):
        inv = 1.0 / (ROPE_THETA ** (jnp.arange(0, dim, 2, dtype=F32) / dim))
        ang = pos * inv[None, :]
        return jnp.cos(ang), jnp.sin(ang)

    c, s = cs(HEAD_DIM)
    cos = jnp.concatenate([c, c], axis=1)
    sin = jnp.concatenate([-s, s], axis=1)
    c, s = cs(IDX_ROPE_DIM)
    one = jnp.ones((seq, LANES - IDX_ROPE_DIM), F32)
    z32 = jnp.zeros_like(s)
    z64 = jnp.zeros_like(one)
    ci = jnp.concatenate([c, c, one], axis=1)
    sa = jnp.concatenate([-s, z32, z64], axis=1)
    sb = jnp.concatenate([z32, s, z64], axis=1)
    return cos, sin, ci, sa, sb


def kernel(x, c, ada_w, ada_b, ffn_w_gate, ffn_w_up, ffn_w_down, a_w_in, a_w_out, a_q_gain, a_k_gain,
           a_lambda_q1, a_lambda_k1, a_lambda_q2, a_lambda_k2, a_subln_gain, b_w_in, b_w_out, b_q_gain,
           b_k_gain, b_kidx_gain, b_kidx_bias):
    b, s, d = x.shape
    depth = ada_w.shape[0]
    hd = HEAD_DIM
    a_heads = d // (2 * hd)
    b_heads = d // hd
    qscale = (hd ** -0.5) * math.log2(math.e)
    topk = min(TOPK_MAX, s // 4)
    tn = 512

    cos, sin, ci, sa, sb = _rope_tables(s)
    mod = _ada_mod(c, ada_w, ada_b)
    wg = ffn_w_gate.astype(BF16)
    wu = ffn_w_up.astype(BF16)
    wd = ffn_w_down.astype(BF16)

    x2 = x.reshape(b * s, d)
    for i in range(depth):
        j = i // 2
        x2 = _ffn(x2, mod[i], wg, wu, wd, i, 0, s)
        if i % 2 == 0:
            lam_init = 0.8 - 0.6 * math.exp(-0.3 * i)
            reps = tn // hd
            gains = jnp.concatenate([
                jnp.tile(jnp.tile(a_q_gain[j] * qscale, reps)[None], (d // tn, 1)),
                jnp.tile(jnp.tile(a_k_gain[j], reps)[None], (d // tn, 1)),
                jnp.ones((d // tn, tn), F32),
            ], axis=0)[:, None, :]
            p = _proj_a(x2, mod[i], a_w_in[j].astype(BF16), gains, cos, sin, s)
            lam_vecs = jnp.stack([a_lambda_q1[j], a_lambda_k1[j], a_lambda_q2[j], a_lambda_k2[j]])
            gain_eff = (a_subln_gain[j] * (1.0 - lam_init))[None, :]
            o = _attn_a(p.reshape(b, s, 3 * d), lam_vecs, gain_eff, a_heads, lam_init)
            x2 = _out_proj(o.reshape(b * s, d), a_w_out[j].astype(BF16), x2, mod[i], s)
        else:
            w = b_w_in[j]
            q_w = b_heads * hd
            qi_w = IDX_HEADS * hd
            o_k, o_v, o_qi, o_ki, o_wi = q_w, q_w + hd, q_w + 2 * hd, q_w + 2 * hd + qi_w, q_w + 3 * hd + qi_w
            w_re = jnp.concatenate([
                w[:, o_qi:o_qi + qi_w], w[:, :q_w], w[:, o_k:o_k + hd], w[:, o_ki:o_ki + hd],
                w[:, o_v:o_v + hd], w[:, o_wi:o_wi + IDX_HEADS],
                jnp.zeros((d, hd - IDX_HEADS), F32),
            ], axis=1).astype(BF16)
            n_tiles = w_re.shape[1] // tn
            reps = tn // hd
            g0 = jnp.zeros((n_tiles, 2, tn), F32)
            g0 = g0.at[qi_w // tn:(qi_w + q_w) // tn, 0, :].set(jnp.tile(b_q_gain[j] * qscale, reps)[None])
            g0 = g0.at[n_tiles - 1, 0, 0:hd].set(b_k_gain[j])
            g0 = g0.at[n_tiles - 1, 0, hd:2 * hd].set(b_kidx_gain[j])
            g0 = g0.at[n_tiles - 1, 1, hd:2 * hd].set(b_kidx_bias[j])
            p, wi = _proj_b(x2, mod[i], w_re, g0, cos, sin, ci, sa, sb, s)
            p3 = p.reshape(b, s, w_re.shape[1])
            v_t = jnp.swapaxes(p3[:, :, qi_w + q_w + 2 * hd:qi_w + q_w + 3 * hd], 1, 2)
            v1t = jnp.concatenate([v_t, jnp.ones_like(v_t)], axis=1)
            wi_t = jnp.swapaxes(wi.reshape(b, s, LANES), 1, 2)
            o = _attn_b(p3, wi_t, v1t, b_heads, topk)
            x2 = _out_proj(o.reshape(b * s, d), b_w_out[j].astype(BF16), x2, mod[i], s)
        x2 = _ffn(x2, mod[i], wg, wu, wd, i, 1, s)
    return x2.reshape(b, s, d)
```

```python
import functools
import math

import jax
import jax.numpy as jnp
import numpy as np
from jax import lax
from jax.experimental import pallas as pl
from jax.experimental.pallas import tpu as pltpu

ROPE_THETA = 10000.0
NORM_EPS = 1e-6
N_ADA = 9
HEAD_DIM = 128
IDX_HEADS = 16
IDX_ROPE_DIM = 64
TOPK_MAX = 256
LANES = 128
ACC_CHAINS = 64
V7X_VMEM_LIMIT = 56 * 1024 * 1024

ROW_TILE = 1024
OUT_ROW_TILE = 512
COL_TILE = 512
PROJ_A_COL_TILE = 1024
ATTN_A_TILE = 1024
ATTN_B_TQ = 256
ATTN_B_TK = 512
NEG = -1e30
INT_MIN = np.int32(-(2 ** 31))

BF16 = jnp.bfloat16
F32 = jnp.float32


def _cparams(sem):
    return pltpu.CompilerParams(dimension_semantics=sem, vmem_limit_bytes=V7X_VMEM_LIMIT)


def _dot(a, b):
    return jnp.dot(a, b, preferred_element_type=F32)


def _dot_nt(a, b):
    return lax.dot_general(a, b, (((1,), (1,)), ((), ())), preferred_element_type=F32)


def _tile_lanes(x, n):
    return x if n == 1 else jnp.concatenate([x] * n, axis=1)


def _modulate(x, shift, scale):
    ms = jnp.mean(x * x, axis=-1, keepdims=True)
    return x * lax.rsqrt(ms + NORM_EPS) * (1.0 + scale) + shift


def _rms(x):
    return x * lax.rsqrt(jnp.mean(x * x, axis=-1, keepdims=True) + NORM_EPS)


def _ada_kernel(c_ref, w_ref, b_ref, o_ref):
    c = c_ref[...]
    ca = (c * (1.0 / (1.0 + jnp.exp(-c)))).astype(BF16)
    o_ref[0] = _dot(ca, w_ref[0].astype(BF16)) + b_ref[0]


def _ada_mod(c, ada_w, ada_b):
    depth, d, n = ada_w.shape
    b = c.shape[0]
    rows = 8
    c8 = jnp.zeros((rows, d), F32).at[:b].set(c)
    tn = math.gcd(n, 1024)
    out = pl.pallas_call(
        _ada_kernel,
        grid=(depth, n // tn),
        in_specs=[
            pl.BlockSpec((rows, d), lambda i, j: (0, 0)),
            pl.BlockSpec((1, d, tn), lambda i, j: (i, 0, j)),
            pl.BlockSpec((1, 1, tn), lambda i, j: (i, 0, j)),
        ],
        out_specs=pl.BlockSpec((1, rows, tn), lambda i, j: (i, 0, j)),
        out_shape=jax.ShapeDtypeStruct((depth, rows, n), F32),
        compiler_params=_cparams(("arbitrary", "arbitrary")),
        name="ada_mod",
    )(c8, ada_w, ada_b.reshape(depth, 1, n))
    return out[:, :b].reshape(depth, b, N_ADA, d)


def _ffn_kernel(x_ref, mod_ref, wg_ref, wu_ref, wd_ref, o_ref, h_ref, *, row0, nf):
    f = pl.program_id(1)

    def down_of_tile():
        h = h_ref[...]
        a = _dot(h, wg_ref[...])
        u = _dot(h, wu_ref[...])
        act = (a * (1.0 / (1.0 + jnp.exp(-a))) * u).astype(BF16)
        return _dot(act, wd_ref[...])

    assert nf >= 2

    @pl.when(f == 0)
    def _():
        h = _modulate(x_ref[...], mod_ref[0, row0:row0 + 1, :], mod_ref[0, row0 + 1:row0 + 2, :])
        h_ref[...] = h.astype(BF16)
        o_ref[...] = down_of_tile()

    @pl.when(jnp.logical_and(f > 0, f < nf - 1))
    def _():
        o_ref[...] += down_of_tile()

    @pl.when(f == nf - 1)
    def _():
        g = mod_ref[0, row0 + 2:row0 + 3, :]
        o_ref[...] = x_ref[...] + 0.5 * g * (o_ref[...] + down_of_tile())


def _ffn(x2, mod, wg, wu, wd, layer, which, seq):
    t, d = x2.shape
    f_dim = wg.shape[3]
    tm = min(ROW_TILE, seq)
    tf = COL_TILE if f_dim % COL_TILE == 0 else f_dim
    nsb = seq // tm
    nf = f_dim // tf
    row0 = 6 * which
    return pl.pallas_call(
        functools.partial(_ffn_kernel, row0=row0, nf=nf),
        grid=(t // tm, nf),
        in_specs=[
            pl.BlockSpec((tm, d), lambda i, f: (i, 0)),
            pl.BlockSpec((1, N_ADA, d), lambda i, f: (i // nsb, 0, 0)),
            pl.BlockSpec((None, None, d, tf), lambda i, f: (layer, which, 0, f)),
            pl.BlockSpec((None, None, d, tf), lambda i, f: (layer, which, 0, f)),
            pl.BlockSpec((None, None, tf, d), lambda i, f: (layer, which, f, 0)),
        ],
        out_specs=pl.BlockSpec((tm, d), lambda i, f: (i, 0)),
        out_shape=jax.ShapeDtypeStruct((t, d), F32),
        scratch_shapes=[pltpu.VMEM((tm, d), BF16)],
        compiler_params=_cparams(("arbitrary", "arbitrary")),
        name="ffn",
    )(x2, mod, wg, wu, wd)


def _rope_full(r, cos, sin):
    return r * cos + pltpu.roll(r, HEAD_DIM // 2, 1) * sin


def _rope_partial(r, ci, sa, sb):
    q = IDX_ROPE_DIM // 2
    return r * ci + pltpu.roll(r, LANES - q, 1) * sa + pltpu.roll(r, q, 1) * sb


def _proj_a_kernel(x_ref, mod_ref, w_ref, gain_ref, cos_ref, sin_ref, o_ref, h_ref, *, n_qk, tn):
    j = pl.program_id(1)

    @pl.when(j == 0)
    def _():
        h = _modulate(x_ref[...], mod_ref[0, 3:4, :], mod_ref[0, 4:5, :])
        h_ref[...] = h.astype(BF16)

    p = _dot(h_ref[...], w_ref[...])

    @pl.when(j < n_qk)
    def _():
        cos = cos_ref[...]
        sin = sin_ref[...]
        for c in range(tn // LANES):
            sl = slice(c * LANES, (c + 1) * LANES)
            r = _rms(p[:, sl]) * gain_ref[0, :, sl]
            o_ref[:, sl] = _rope_full(r, cos, sin).astype(BF16)

    @pl.when(j >= n_qk)
    def _():
        o_ref[...] = p.astype(BF16)


def _proj_a(x2, mod, w, gains, cos, sin, seq):
    t, d = x2.shape
    n = w.shape[1]
    tm = min(ROW_TILE, seq)
    tn = gains.shape[2]
    nsb = seq // tm
    n_qk = 2 * d // tn
    return pl.pallas_call(
        functools.partial(_proj_a_kernel, n_qk=n_qk, tn=tn),
        grid=(t // tm, n // tn),
        in_specs=[
            pl.BlockSpec((tm, d), lambda i, j: (i, 0)),
            pl.BlockSpec((1, N_ADA, d), lambda i, j: (i // nsb, 0, 0)),
            pl.BlockSpec((d, tn), lambda i, j: (0, j)),
            pl.BlockSpec((1, 1, tn), lambda i, j: (j, 0, 0)),
            pl.BlockSpec((tm, LANES), lambda i, j: (i % nsb, 0)),
            pl.BlockSpec((tm, LANES), lambda i, j: (i % nsb, 0)),
        ],
        out_specs=pl.BlockSpec((tm, tn), lambda i, j: (i, j)),
        out_shape=jax.ShapeDtypeStruct((t, n), BF16),
        scratch_shapes=[pltpu.VMEM((tm, d), BF16)],
        compiler_params=_cparams(("arbitrary", "arbitrary")),
        name="proj_a",
    )(x2, mod, w, gains, cos, sin)


def _proj_b_kernel(x_ref, mod_ref, w_ref, gain_ref, cos_ref, sin_ref, ci_ref, sa_ref, sb_ref,
                   o_ref, wi_ref, h_ref, *, n_q, n_qi, tn, wi_scale):
    j = pl.program_id(1)

    @pl.when(j == 0)
    def _():
        h = _modulate(x_ref[...], mod_ref[0, 3:4, :], mod_ref[0, 4:5, :])
        h_ref[...] = h.astype(BF16)

    p = _dot(h_ref[...], w_ref[...])

    @pl.when(j < n_qi)
    def _():
        ci = ci_ref[...]
        sa = sa_ref[...]
        sb = sb_ref[...]
        for c in range(tn // LANES):
            sl = slice(c * LANES, (c + 1) * LANES)
            o_ref[:, sl] = _rope_partial(p[:, sl], ci, sa, sb).astype(BF16)

    @pl.when(jnp.logical_and(j >= n_qi, j < n_q + n_qi))
    def _():
        cos = cos_ref[...]
        sin = sin_ref[...]
        for c in range(tn // LANES):
            sl = slice(c * LANES, (c + 1) * LANES)
            r = _rms(p[:, sl]) * gain_ref[0, 0:1, sl]
            o_ref[:, sl] = _rope_full(r, cos, sin).astype(BF16)

    @pl.when(j == n_q + n_qi)
    def _():
        k = _rms(p[:, 0:LANES]) * gain_ref[0, 0:1, 0:LANES]
        o_ref[:, 0:LANES] = _rope_full(k, cos_ref[...], sin_ref[...]).astype(BF16)
        ki = p[:, LANES:2 * LANES]
        mu = jnp.mean(ki, axis=-1, keepdims=True)
        kc = ki - mu
        var = jnp.mean(kc * kc, axis=-1, keepdims=True)
        kn = kc * lax.rsqrt(var + NORM_EPS) * gain_ref[0, 0:1, LANES:2 * LANES] \
            + gain_ref[0, 1:2, LANES:2 * LANES]
        o_ref[:, LANES:2 * LANES] = _rope_partial(kn, ci_ref[...], sa_ref[...], sb_ref[...]).astype(BF16)
        o_ref[:, 2 * LANES:3 * LANES] = p[:, 2 * LANES:3 * LANES].astype(BF16)
        wi = p[:, 3 * LANES:4 * LANES] * wi_scale
        o_ref[:, 3 * LANES:4 * LANES] = wi.astype(BF16)
        wi_ref[...] = wi


def _proj_b(x2, mod, w, gains, cos, sin, ci, sa, sb, seq):
    t, d = x2.shape
    n = w.shape[1]
    tm = min(ROW_TILE, seq)
    tn = COL_TILE
    nsb = seq // tm
    n_q = d // tn
    n_qi = IDX_HEADS * HEAD_DIM // tn
    assert n == (n_q + n_qi + 1) * tn
    wi_scale = (IDX_HEADS ** -0.5) * (HEAD_DIM ** -0.5)
    tab = pl.BlockSpec((tm, LANES), lambda i, j: (i % nsb, 0))
    return pl.pallas_call(
        functools.partial(_proj_b_kernel, n_q=n_q, n_qi=n_qi, tn=tn, wi_scale=wi_scale),
        grid=(t // tm, n // tn),
        in_specs=[
            pl.BlockSpec((tm, d), lambda i, j: (i, 0)),
            pl.BlockSpec((1, N_ADA, d), lambda i, j: (i // nsb, 0, 0)),
            pl.BlockSpec((d, tn), lambda i, j: (0, j)),
            pl.BlockSpec((1, 2, tn), lambda i, j: (j, 0, 0)),
            tab, tab, tab, tab, tab,
        ],
        out_specs=[
            pl.BlockSpec((tm, tn), lambda i, j: (i, j)),
            pl.BlockSpec((tm, LANES), lambda i, j: (i, 0)),
        ],
        out_shape=[
            jax.ShapeDtypeStruct((t, n), BF16),
            jax.ShapeDtypeStruct((t, LANES), F32),
        ],
        scratch_shapes=[pltpu.VMEM((tm, d), BF16)],
        compiler_params=_cparams(("arbitrary", "arbitrary")),
        name="proj_b",
    )(x2, mod, w, gains, cos, sin, ci, sa, sb)


def _out_kernel(o_ref, w_ref, x_ref, mod_ref, y_ref):
    y = _dot(o_ref[...], w_ref[...])
    y_ref[...] = x_ref[...] + mod_ref[0, 5:6, :] * y


def _out_proj(o2, w, x2, mod, seq):
    t, d = x2.shape
    k = o2.shape[1]
    tm = min(OUT_ROW_TILE, seq)
    nsb = seq // tm
    return pl.pallas_call(
        _out_kernel,
        grid=(t // tm,),
        in_specs=[
            pl.BlockSpec((tm, k), lambda i: (i, 0)),
            pl.BlockSpec((k, d), lambda i: (0, 0)),
            pl.BlockSpec((tm, d), lambda i: (i, 0)),
            pl.BlockSpec((1, N_ADA, d), lambda i: (i // nsb, 0, 0)),
        ],
        out_specs=pl.BlockSpec((tm, d), lambda i: (i, 0)),
        out_shape=jax.ShapeDtypeStruct((t, d), F32),
        compiler_params=_cparams(("arbitrary",)),
        name="out_proj",
    )(o2, w, x2, mod)


def _softmax_step(s, m_ref, l_ref, acc_ref, v, idx, rows=slice(None)):
    tk = s.shape[1]
    dv = v.shape[1]
    m_prev = m_ref[idx, rows, :]
    m_new = jnp.maximum(m_prev, jnp.max(s, axis=1, keepdims=True))
    alpha = jnp.exp2(m_prev - m_new)
    p = jnp.exp2(s - _tile_lanes(m_new, tk // LANES))
    l_ref[idx, rows, :] = alpha * l_ref[idx, rows, :] + jnp.sum(p, axis=1, keepdims=True)
    acc_ref[idx, rows, :] = _tile_lanes(alpha, dv // LANES) * acc_ref[idx, rows, :] + _dot(p.astype(BF16), v)
    m_ref[idx, rows, :] = m_new


def _attn_a_kernel(q0_ref, q1_ref, k0_ref, k1_ref, v_ref, lam_ref, gain_ref, o_ref,
                   m_ref, l_ref, acc_ref, *, tq, lam_init):
    qi = pl.program_id(2)
    m_ref[...] = jnp.full_like(m_ref, NEG)
    l_ref[...] = jnp.zeros_like(l_ref)
    acc_ref[...] = jnp.zeros_like(acc_ref)
    tk = tq
    q0 = q0_ref[0]
    q1 = q1_ref[0]

    def body(c, carry):
        off = pl.multiple_of(c * tk, tk)
        v = v_ref[0, pl.ds(off, tk), :]
        s0 = _dot_nt(q0, k0_ref[0, pl.ds(off, tk), :])
        s1 = _dot_nt(q1, k1_ref[0, pl.ds(off, tk), :])
        _softmax_step(s0, m_ref, l_ref, acc_ref, v, 0)
        _softmax_step(s1, m_ref, l_ref, acc_ref, v, 1)
        return carry

    lax.fori_loop(0, qi, body, 0)

    off = pl.multiple_of(qi * tk, tk)
    band = tq // 2
    for r in range(2):
        rows = slice(r * band, (r + 1) * band)
        width = (r + 1) * band
        v = v_ref[0, pl.ds(off, width), :]
        row = r * band + lax.broadcasted_iota(jnp.int32, (band, width), 0)
        keep = lax.broadcasted_iota(jnp.int32, (band, width), 1) <= row
        s0 = _dot_nt(q0_ref[0, rows, :], k0_ref[0, pl.ds(off, width), :])
        s1 = _dot_nt(q1_ref[0, rows, :], k1_ref[0, pl.ds(off, width), :])
        _softmax_step(jnp.where(keep, s0, NEG), m_ref, l_ref, acc_ref, v, 0, rows)
        _softmax_step(jnp.where(keep, s1, NEG), m_ref, l_ref, acc_ref, v, 1, rows)

    dv = acc_ref.shape[2]
    lam_q = lam_ref[...]
    s1 = jnp.sum(lam_q[0:1, :] * lam_q[1:2, :], axis=1, keepdims=True)
    s2 = jnp.sum(lam_q[2:3, :] * lam_q[3:4, :], axis=1, keepdims=True)
    lam = jnp.exp(s1) - jnp.exp(s2) + lam_init
    o0 = acc_ref[0] * _tile_lanes(1.0 / l_ref[0], dv // LANES)
    o1 = acc_ref[1] * _tile_lanes(1.0 / l_ref[1], dv // LANES)
    o = o0 - lam * o1
    o_ref[0] = (_rms(o) * gain_ref[...]).astype(BF16)


def _attn_a(p3, lam_vecs, gain_eff, n_heads, lam_init):
    b, s, n = p3.shape
    d = n // 3
    tq = min(ATTN_A_TILE, s)
    hd = HEAD_DIM
    kcol = d // hd
    vcol = 2 * d // (2 * hd)
    return pl.pallas_call(
        functools.partial(_attn_a_kernel, tq=tq, lam_init=lam_init),
        grid=(b, n_heads, s // tq),
        in_specs=[
            pl.BlockSpec((1, tq, hd), lambda bi, h, i: (bi, i, 2 * h)),
            pl.BlockSpec((1, tq, hd), lambda bi, h, i: (bi, i, 2 * h + 1)),
            pl.BlockSpec((1, s, hd), lambda bi, h, i: (bi, 0, kcol + 2 * h)),
            pl.BlockSpec((1, s, hd), lambda bi, h, i: (bi, 0, kcol + 2 * h + 1)),
            pl.BlockSpec((1, s, 2 * hd), lambda bi, h, i: (bi, 0, vcol + h)),
            pl.BlockSpec((4, hd), lambda bi, h, i: (0, 0)),
            pl.BlockSpec((1, 2 * hd), lambda bi, h, i: (0, 0)),
        ],
        out_specs=pl.BlockSpec((1, tq, 2 * hd), lambda bi, h, i: (bi, i, h)),
        out_shape=jax.ShapeDtypeStruct((b, s, d), BF16),
        scratch_shapes=[
            pltpu.VMEM((2, tq, LANES), F32),
            pltpu.VMEM((2, tq, LANES), F32),
            pltpu.VMEM((2, tq, 2 * hd), F32),
        ],
        compiler_params=_cparams(("arbitrary", "arbitrary", "arbitrary")),
        name="attn_a",
    )(p3, p3, p3, p3, p3, lam_vecs, gain_eff)


def _attn_b_kernel(q_ref, qi_ref, wi_ref, k_ref, ki_ref, vt_ref, o_ref,
                   keys_ref, m_ref, acc_ref, *, tq, tk, n_heads, topk, idx_bits):
    i = pl.program_id(1)
    n_keys = (i + 1) * tq
    n_full = n_keys // tk
    has_tail = n_keys % tk != 0
    nkc = (n_keys + tk - 1) // tk
    kf = float(topk)

    def key_pos(off, rows):
        return off + lax.broadcasted_iota(jnp.int32, (rows, tq), 0)

    def score_block(off, width):
        kic = ki_ref[0, pl.ds(off, width), :]
        sc = jnp.zeros((width, tq), F32)
        for h in range(IDX_HEADS):
            s = _dot_nt(kic, qi_ref[0, :, h * HEAD_DIM:(h + 1) * HEAD_DIM])
            sc = sc + wi_ref[0, h:h + 1, :] * jnp.maximum(s, 0.0)
        bits = pltpu.bitcast(sc, jnp.int32)
        key = bits ^ ((bits >> 31) & jnp.int32(0x7FFFFFFF))
        qpos = i * tq + lax.broadcasted_iota(jnp.int32, (width, tq), 1)
        keys_ref[pl.ds(off, width), :] = jnp.where(key_pos(off, width) <= qpos, key, INT_MIN)

    def score_pair(j, carry):
        score_block(pl.multiple_of(2 * j * tk, tk), tk)
        score_block(pl.multiple_of((2 * j + 1) * tk, tk), tk)
        return carry

    lax.fori_loop(0, n_full // 2, score_pair, 0)

    @pl.when(n_full % 2 == 1)
    def _():
        score_block(pl.multiple_of((n_full - 1) * tk, tk), tk)

    if tk > tq:
        @pl.when(has_tail)
        def _():
            off = pl.multiple_of(n_full * tk, tk)
            score_block(off, tq)
            keys_ref[pl.ds(off + tq, tk - tq), :] = jnp.full((tk - tq, tq), INT_MIN, jnp.int32)

    def col_total(part):
        return jnp.sum(part.astype(F32), axis=0, keepdims=True)

    def fold_chunk(off, part, hit_fn):
        for r in range(tk // ACC_CHAINS):
            ro = off + r * ACC_CHAINS
            part = part + jnp.where(hit_fn(keys_ref[pl.ds(ro, ACC_CHAINS), :], ro), 1, 0)
        return part

    def count_ge(thr):
        def cbody(c, part):
            off = pl.multiple_of(c * tk, tk)
            return fold_chunk(off, part, lambda kk, ro: kk >= thr)
        return col_total(lax.fori_loop(0, nkc, cbody, jnp.zeros((ACC_CHAINS, tq), jnp.int32)))

    def search(it, thr):
        cand = thr + jnp.left_shift(jnp.int32(1), 31 - it)
        return jnp.where(count_ge(cand) >= kf, cand, thr)

    thr = lax.fori_loop(0, 32, search, jnp.full((1, tq), INT_MIN, jnp.int32))

    tie = jnp.where(jnp.logical_and(count_ge(thr) > kf, thr > INT_MIN), 1.0, 0.0)
    any_tie = jnp.max(tie)

    @pl.when(any_tie > 0.0)
    def _():
        need = kf - count_ge(thr + 1)

        def count_eq_before(jcut):
            def cbody(c, part):
                off = pl.multiple_of(c * tk, tk)
                return fold_chunk(off, part, lambda kk, ro: jnp.logical_and(
                    kk == thr, key_pos(ro, ACC_CHAINS) < jcut))
            return col_total(lax.fori_loop(0, nkc, cbody, jnp.zeros((ACC_CHAINS, tq), jnp.int32)))

        def jsearch(it, jcut):
            cand = jcut + jnp.left_shift(jnp.int32(1), idx_bits - 1 - it)
            return jnp.where(count_eq_before(cand) <= need, cand, jcut)

        jcut = lax.fori_loop(0, idx_bits, jsearch, jnp.zeros((1, tq), jnp.int32))

        def demote(c, carry):
            off = pl.multiple_of(c * tk, tk)
            kk = keys_ref[pl.ds(off, tk), :]
            drop = jnp.logical_and(jnp.logical_and(kk == thr, key_pos(off, tk) >= jcut), tie > 0.0)
            keys_ref[pl.ds(off, tk), :] = jnp.where(drop, kk - 1, kk)
            return carry

        lax.fori_loop(0, nkc, demote, 0)

    thr_sel = jnp.maximum(thr, INT_MIN + 1)
    m_ref[...] = jnp.full_like(m_ref, NEG)
    acc_ref[...] = jnp.zeros_like(acc_ref)

    def attend(c, width=tk):
        off = pl.multiple_of(c * tk, tk)
        kc = k_ref[0, pl.ds(off, width), :]
        vt = vt_ref[0, :, pl.ds(off, width)]
        bias = jnp.where(keys_ref[pl.ds(off, width), :] >= thr_sel, 0.0, NEG)
        for h in range(n_heads):
            s = _dot_nt(kc, q_ref[0, :, h * HEAD_DIM:(h + 1) * HEAD_DIM]) + bias
            m_prev = m_ref[h]
            m_new = jnp.maximum(m_prev, jnp.max(s, axis=0, keepdims=True))
            alpha = jnp.exp2(m_prev - m_new)
            p = jnp.exp2(s - m_new)
            acc_ref[h] = alpha * acc_ref[h] + _dot(vt, p.astype(BF16))
            m_ref[h] = m_new

    def attend_quad(j, carry):
        for u in range(4):
            attend(4 * j + u)
        return carry

    lax.fori_loop(0, n_full // 4, attend_quad, 0)

    @pl.when(n_full % 4 >= 2)
    def _():
        attend((n_full // 4) * 4)
        attend((n_full // 4) * 4 + 1)

    @pl.when(n_full % 2 == 1)
    def _():
        attend(n_full - 1)

    if tk > tq:
        @pl.when(has_tail)
        def _():
            attend(n_full, tq)

    for h in range(n_heads):
        a = acc_ref[h]
        o = a[0:HEAD_DIM, :] * (1.0 / a[HEAD_DIM:HEAD_DIM + 1, :])
        o_ref[0, :, h * HEAD_DIM:(h + 1) * HEAD_DIM] = o.T.astype(BF16)


def _attn_b(p3, wi_t, v1t, n_heads, topk):
    b, s, n = p3.shape
    tq = min(ATTN_B_TQ, s)
    tk = min(ATTN_B_TK, s)
    d = n_heads * HEAD_DIM
    qi_w = IDX_HEADS * HEAD_DIM
    assert qi_w % d == 0
    kcol = (d + qi_w) // HEAD_DIM
    vrows = v1t.shape[1]
    assert vrows == 2 * HEAD_DIM
    assert tk in (tq, 2 * tq)
    return pl.pallas_call(
        functools.partial(_attn_b_kernel, tq=tq, tk=tk, n_heads=n_heads, topk=topk,
                          idx_bits=int(s).bit_length()),
        grid=(b, s // tq),
        in_specs=[
            pl.BlockSpec((1, tq, d), lambda bi, i: (bi, i, qi_w // d)),
            pl.BlockSpec((1, tq, qi_w), lambda bi, i: (bi, i, 0)),
            pl.BlockSpec((1, LANES, tq), lambda bi, i: (bi, 0, i)),
            pl.BlockSpec((1, s, HEAD_DIM), lambda bi, i: (bi, 0, kcol)),
            pl.BlockSpec((1, s, HEAD_DIM), lambda bi, i: (bi, 0, kcol + 1)),
            pl.BlockSpec((1, vrows, s), lambda bi, i: (bi, 0, 0)),
        ],
        out_specs=pl.BlockSpec((1, tq, d), lambda bi, i: (bi, i, 0)),
        out_shape=jax.ShapeDtypeStruct((b, s, d), BF16),
        scratch_shapes=[
            pltpu.VMEM((s, tq), jnp.int32),
            pltpu.VMEM((n_heads, 1, tq), F32),
            pltpu.VMEM((n_heads, vrows, tq), F32),
        ],
        compiler_params=_cparams(("arbitrary", "arbitrary")),
        name="attn_b",
    )(p3, p3, wi_t, p3, p3, v1t)


def _rope_tables(seq):
    pos = jnp.arange(seq, dtype=F32)[:, None]

    def cs(dim):
        inv = 1.0 / (ROPE_THETA ** (jnp.arange(0, dim, 2, dtype=F32) / dim))
        ang = pos * inv[None, :]
        return jnp.cos(ang), jnp.sin(ang)

    c, s = cs(HEAD_DIM)
    cos = jnp.concatenate([c, c], axis=1)
    sin = jnp.concatenate([-s, s], axis=1)
    c, s = cs(IDX_ROPE_DIM)
    one = jnp.ones((seq, LANES - IDX_ROPE_DIM), F32)
    z32 = jnp.zeros_like(s)
    z64 = jnp.zeros_like(one)
    ci = jnp.concatenate([c, c, one], axis=1)
    sa = jnp.concatenate([-s, z32, z64], axis=1)
    sb = jnp.concatenate([z32, s, z64], axis=1)
    return cos, sin, ci, sa, sb


def kernel(x, c, ada_w, ada_b, ffn_w_gate, ffn_w_up, ffn_w_down, a_w_in, a_w_out, a_q_gain, a_k_gain,
           a_lambda_q1, a_lambda_k1, a_lambda_q2, a_lambda_k2, a_subln_gain, b_w_in, b_w_out, b_q_gain,
           b_k_gain, b_kidx_gain, b_kidx_bias):
    b, s, d = x.shape
    depth = ada_w.shape[0]
    hd = HEAD_DIM
    a_heads = d // (2 * hd)
    b_heads = d // hd
    qscale = (hd ** -0.5) * math.log2(math.e)
    topk = min(TOPK_MAX, s // 4)
    tn = COL_TILE

    cos, sin, ci, sa, sb = _rope_tables(s)
    mod = _ada_mod(c, ada_w, ada_b)
    wg = ffn_w_gate.astype(BF16)
    wu = ffn_w_up.astype(BF16)
    wd = ffn_w_down.astype(BF16)

    x2 = x.reshape(b * s, d)
    for i in range(depth):
        j = i // 2
        x2 = _ffn(x2, mod[i], wg, wu, wd, i, 0, s)
        if i % 2 == 0:
            lam_init = 0.8 - 0.6 * math.exp(-0.3 * i)
            tn_a = PROJ_A_COL_TILE if d % PROJ_A_COL_TILE == 0 else tn
            reps = tn_a // hd
            gains = jnp.concatenate([
                jnp.tile(jnp.tile(a_q_gain[j] * qscale, reps)[None], (d // tn_a, 1)),
                jnp.tile(jnp.tile(a_k_gain[j], reps)[None], (d // tn_a, 1)),
                jnp.ones((d // tn_a, tn_a), F32),
            ], axis=0)[:, None, :]
            p = _proj_a(x2, mod[i], a_w_in[j].astype(BF16), gains, cos, sin, s)
            lam_vecs = jnp.stack([a_lambda_q1[j], a_lambda_k1[j], a_lambda_q2[j], a_lambda_k2[j]])
            gain_eff = (a_subln_gain[j] * (1.0 - lam_init))[None, :]
            o = _attn_a(p.reshape(b, s, 3 * d), lam_vecs, gain_eff, a_heads, lam_init)
            x2 = _out_proj(o.reshape(b * s, d), a_w_out[j].astype(BF16), x2, mod[i], s)
        else:
            w = b_w_in[j]
            q_w = b_heads * hd
            qi_w = IDX_HEADS * hd
            o_k, o_v, o_qi, o_ki, o_wi = q_w, q_w + hd, q_w + 2 * hd, q_w + 2 * hd + qi_w, q_w + 3 * hd + qi_w
            w_re = jnp.concatenate([
                w[:, o_qi:o_qi + qi_w], w[:, :q_w], w[:, o_k:o_k + hd], w[:, o_ki:o_ki + hd],
                w[:, o_v:o_v + hd], w[:, o_wi:o_wi + IDX_HEADS],
                jnp.zeros((d, hd - IDX_HEADS), F32),
            ], axis=1).astype(BF16)
            n_tiles = w_re.shape[1] // tn
            reps = tn // hd
            g0 = jnp.zeros((n_tiles, 2, tn), F32)
            g0 = g0.at[qi_w // tn:(qi_w + q_w) // tn, 0, :].set(jnp.tile(b_q_gain[j] * qscale, reps)[None])
            g0 = g0.at[n_tiles - 1, 0, 0:hd].set(b_k_gain[j])
            g0 = g0.at[n_tiles - 1, 0, hd:2 * hd].set(b_kidx_gain[j])
            g0 = g0.at[n_tiles - 1, 1, hd:2 * hd].set(b_kidx_bias[j])
            p, wi = _proj_b(x2, mod[i], w_re, g0, cos, sin, ci, sa, sb, s)
            p3 = p.reshape(b, s, w_re.shape[1])
            v_t = jnp.swapaxes(p3[:, :, qi_w + q_w + 2 * hd:qi_w + q_w + 3 * hd], 1, 2)
            v1t = jnp.concatenate([v_t, jnp.ones_like(v_t)], axis=1)
            wi_t = jnp.swapaxes(wi.reshape(b, s, LANES), 1, 2)
            o = _attn_b(p3, wi_t, v1t, b_heads, topk)
            x2 = _out_proj(o.reshape(b * s, d), b_w_out[j].astype(BF16), x2, mod[i], s)
        x2 = _ffn(x2, mod[i], wg, wu, wd, i, 1, s)
    return x2.reshape(b, s, d)
```
